```python
import jax, jax.numpy as jnp
from jax import lax
import numpy as np

D_MODEL = 1024
BATCH = 4
SEQ = 8192
DEPTH = 1

HEAD_DIM = 64
N_HEADS = D_MODEL // HEAD_DIM
N_SB_HEADS = N_HEADS // 2
N_DIL_HEADS = N_HEADS - N_SB_HEADS
SB_WIDTH = N_SB_HEADS * HEAD_DIM
DIL_WIDTH = N_DIL_HEADS * HEAD_DIM
MIX_WIDTH = SB_WIDTH + DIL_WIDTH
IN_WIDTH = 3 * SB_WIDTH + 3 * DIL_WIDTH
DILATED_PATTERNS = ((128, 1), (512, 4), (2048, 16))
Q_BLOCK = 128
D_FF = ((8 * D_MODEL + 3 * 256 - 1) // (3 * 256)) * 256
ALPHA = (2.0 * DEPTH) ** 0.25
BETA = (8.0 * DEPTH) ** -0.25
LN_EPS = 1e-5
RMS_EPS = 1e-6

kernel_name = "stickbreak_dilated_hybrid_deepnorm"


def _layer_norm(x, g, b):
    xf = x.astype(jnp.float32)
    mu = jnp.mean(xf, axis=-1, keepdims=True)
    var = jnp.mean(jnp.square(xf - mu), axis=-1, keepdims=True)
    y = (xf - mu) * lax.rsqrt(var + LN_EPS) * g.astype(jnp.float32) + b.astype(jnp.float32)
    return y.astype(x.dtype)


def _heads(t):
    b, s, w = t.shape
    return t.reshape(b, s, w // HEAD_DIM, HEAD_DIM).transpose(0, 2, 1, 3)


def _head_rms_merge(o, gain):
    of = o.astype(jnp.float32)
    of = of * lax.rsqrt(jnp.mean(jnp.square(of), axis=-1, keepdims=True) + RMS_EPS)
    b, h, s, dh = o.shape
    of = of.transpose(0, 2, 1, 3).reshape(b, s, h * dh)
    return (of * gain.astype(jnp.float32)).astype(o.dtype)


def _alibi_slopes(n_heads):
    return jnp.exp2(-8.0 * (jnp.arange(n_heads, dtype=jnp.float32) + 1.0) / n_heads)


def stick_breaking_attention(q, k, v):
    b, h, s, dh = q.shape
    nb = s // Q_BLOCK
    scale = dh ** -0.5
    key_pos = jnp.arange(s)
    q_blocks = q.reshape(b, h, nb, Q_BLOCK, dh).transpose(2, 0, 1, 3, 4)

    def block(args):
        q_blk, i = args
        q_pos = i * Q_BLOCK + jnp.arange(Q_BLOCK)
        z = jnp.einsum('bhqd,bhkd->bhqk', q_blk, k).astype(jnp.float32) * scale
        mask = key_pos[None, :] < q_pos[:, None]
        log_stay = jnp.where(mask, jax.nn.log_sigmoid(-z), 0.0)
        after = lax.cumsum(log_stay, axis=3, reverse=True) - log_stay
        w = jnp.where(mask, jnp.exp(jax.nn.log_sigmoid(z) + after), 0.0)
        return jnp.einsum('bhqk,bhkd->bhqd', w.astype(v.dtype), v)

    out = lax.map(block, (q_blocks, jnp.arange(nb)))
    return out.transpose(1, 2, 0, 3, 4).reshape(b, h, s, dh)


def dilated_window_attention(q, k, v, slopes):
    b, h, s, dh = q.shape
    nb = s // Q_BLOCK
    scale = dh ** -0.5
    q_blocks = q.reshape(b, h, nb, Q_BLOCK, dh).transpose(2, 0, 1, 3, 4)
    slopes_f = slopes.astype(jnp.float32)[None, :, None, None]

    def block(args):
        q_blk, i = args
        q_pos = i * Q_BLOCK + jnp.arange(Q_BLOCK)
        outs, maxes, denoms = [], [], []
        for window, dil in DILATED_PATTERNS:
            dist = jnp.arange(window // dil + 1) * dil
            k_pos = q_pos[:, None] - dist[None, :]
            valid = k_pos >= 0
            idx = jnp.maximum(k_pos, 0)
            k_g = jnp.take(k, idx, axis=2)
            v_g = jnp.take(v, idx, axis=2)
            sc = jnp.einsum('bhqd,bhqjd->bhqj', q_blk, k_g).astype(jnp.float32) * scale
            sc = sc - slopes_f * dist.astype(jnp.float32)[None, None, None, :]
            sc = jnp.where(valid, sc, -jnp.inf)
            m = jnp.max(sc, axis=-1, keepdims=True)
            p = jnp.exp(sc - m)
            l = jnp.sum(p, axis=-1, keepdims=True)
            outs.append(jnp.einsum('bhqj,bhqjd->bhqd', (p / l).astype(v.dtype), v_g).astype(jnp.float32))
            maxes.append(m)
            denoms.append(l)
        m_all = jnp.stack(maxes)
        l_all = jnp.stack(denoms)
        wts = l_all * jnp.exp(m_all - jnp.max(m_all, axis=0, keepdims=True))
        wts = wts / jnp.sum(wts, axis=0, keepdims=True)
        return jnp.sum(wts * jnp.stack(outs), axis=0).astype(q.dtype)

    out = lax.map(block, (q_blocks, jnp.arange(nb)))
    return out.transpose(1, 2, 0, 3, 4).reshape(b, h, s, dh)


def setup_inputs(seed: int = 0) -> dict:
    key = jax.random.key(seed)
    ks = jax.random.split(key, 12)
    f32 = jnp.float32
    x = jax.random.normal(ks[0], (BATCH, SEQ, D_MODEL), f32)
    col_scale = jnp.concatenate([
        jnp.ones((2 * SB_WIDTH,), f32), jnp.full((SB_WIDTH,), BETA, f32),
        jnp.ones((2 * DIL_WIDTH,), f32), jnp.full((DIL_WIDTH,), BETA, f32)])
    w_in = jax.random.normal(ks[1], (DEPTH, D_MODEL, IN_WIDTH), f32) * (D_MODEL ** -0.5) * col_scale
    g_sb = 1.0 + 0.02 * jax.random.normal(ks[2], (DEPTH, SB_WIDTH), f32)
    g_dil = 1.0 + 0.02 * jax.random.normal(ks[3], (DEPTH, DIL_WIDTH), f32)
    w_out = jax.random.normal(ks[4], (DEPTH, MIX_WIDTH, D_MODEL), f32) * (MIX_WIDTH ** -0.5) * BETA
    ln1_g = 1.0 + 0.02 * jax.random.normal(ks[5], (DEPTH, D_MODEL), f32)
    ln1_b = 0.02 * jax.random.normal(ks[6], (DEPTH, D_MODEL), f32)
    w_gate = jax.random.normal(ks[7], (DEPTH, D_MODEL, D_FF), f32) * (D_MODEL ** -0.5) * BETA
    w_up = jax.random.normal(ks[8], (DEPTH, D_MODEL, D_FF), f32) * (D_MODEL ** -0.5) * BETA
    w_down = jax.random.normal(ks[9], (DEPTH, D_FF, D_MODEL), f32) * (D_FF ** -0.5) * BETA
    ln2_g = 1.0 + 0.02 * jax.random.normal(ks[10], (DEPTH, D_MODEL), f32)
    ln2_b = 0.02 * jax.random.normal(ks[11], (DEPTH, D_MODEL), f32)
    return {"x": x, "w_in": w_in, "g_sb": g_sb, "g_dil": g_dil, "w_out": w_out,
            "ln1_g": ln1_g, "ln1_b": ln1_b, "w_gate": w_gate, "w_up": w_up,
            "w_down": w_down, "ln2_g": ln2_g, "ln2_b": ln2_b}


def reference(x, w_in, g_sb, g_dil, w_out, ln1_g, ln1_b, w_gate, w_up, w_down, ln2_g, ln2_b):
    slopes = _alibi_slopes(N_DIL_HEADS)
    split_at = [SB_WIDTH, 2 * SB_WIDTH, 3 * SB_WIDTH,
                3 * SB_WIDTH + DIL_WIDTH, 3 * SB_WIDTH + 2 * DIL_WIDTH]
    h = x
    for l in range(DEPTH):
        proj = jnp.einsum('bsd,dn->bsn', h, w_in[l])
        q_a, k_a, v_a, q_b, k_b, v_b = jnp.split(proj, split_at, axis=-1)
        o_a = stick_breaking_attention(_heads(q_a), _heads(k_a), _heads(v_a))
        o_b = dilated_window_attention(_heads(q_b), _heads(k_b), _heads(v_b), slopes)
        mixed = jnp.concatenate([_head_rms_merge(o_a, g_sb[l]),
                                 _head_rms_merge(o_b, g_dil[l])], axis=-1)
        mix_out = jnp.einsum('bsm,md->bsd', mixed, w_out[l])
        h = _layer_norm(ALPHA * h + mix_out, ln1_g[l], ln1_b[l])
        gate = jnp.einsum('bsd,df->bsf', h, w_gate[l])
        up = jnp.einsum('bsd,df->bsf', h, w_up[l])
        ffn = jnp.einsum('bsf,fd->bsd', jax.nn.silu(gate) * up, w_down[l])
        h = _layer_norm(ALPHA * h + ffn, ln2_g[l], ln2_b[l])
    return h
```

```python
import functools
import math

import jax
import jax.numpy as jnp
from jax import lax
from jax.experimental import pallas as pl
from jax.experimental.pallas import tpu as pltpu

HEAD_DIM = 64
LANES = 128
PAIR = LANES // HEAD_DIM
DILATED_PATTERNS = ((128, 1), (512, 4), (2048, 16))
WINDOW_KEYS = 128
LN_EPS = 1e-5
RMS_EPS = 1e-6
NEG_BIG = -1e30
VMEM_LIMIT = 48 * 1024 * 1024

F32 = jnp.float32
BF16 = jnp.bfloat16


def _dot(a, b):
    return jnp.dot(a, b, preferred_element_type=F32)


def _dot_nt(a, b):
    return lax.dot_general(a, b, (((1,), (1,)), ((), ())), preferred_element_type=F32)


def _split_dot(x, m):
    hi = x.astype(BF16)
    lo = (x - hi.astype(F32)).astype(BF16)
    return _dot(hi, m) + _dot(lo, m)


def _layer_norm(y, g, b):
    mu = jnp.mean(y, axis=-1, keepdims=True)
    d = y - mu
    var = jnp.mean(d * d, axis=-1, keepdims=True)
    return d * lax.rsqrt(var + LN_EPS) * g + b


def _head_rms(o, group_ones, gain):
    ms = _split_dot(o * o, group_ones) * (1.0 / HEAD_DIM)
    return o * lax.rsqrt(ms + RMS_EPS) * gain


def _params(sem):
    return pltpu.CompilerParams(dimension_semantics=sem, vmem_limit_bytes=VMEM_LIMIT)


def _in_proj_kernel(x_ref, w_ref, o_ref):
    o_ref[...] = _dot(x_ref[...].astype(BF16), w_ref[...]).astype(o_ref.dtype)


def _in_proj(x2d, w, tm):
    n, d = x2d.shape
    width = w.shape[1]
    return pl.pallas_call(
        _in_proj_kernel,
        grid=(n // tm,),
        in_specs=[pl.BlockSpec((tm, d), lambda i: (i, 0)),
                  pl.BlockSpec((d, width), lambda i: (0, 0))],
        out_specs=pl.BlockSpec((tm, width), lambda i: (i, 0)),
        out_shape=jax.ShapeDtypeStruct((n, width), BF16),
        compiler_params=_params(("parallel",)),
        name="in_proj",
    )(x2d, w)


def _sb_kernel(q_ref, k_ref, v_ref, tri_ref, ones_ref, g_ref, o_ref, acc_ref, carry_ref, *, tq):
    i = pl.program_id(2)
    lane = lax.broadcasted_iota(jnp.int32, (tq, LANES), 1)
    q = q_ref[0] * jnp.asarray(HEAD_DIM ** -0.5, BF16)
    zero = jnp.zeros_like(q)
    q2 = jnp.concatenate([jnp.where(lane < HEAD_DIM, q, zero),
                          jnp.where(lane < HEAD_DIM, zero, q)], axis=0)
    tri = tri_ref[...]

    acc_ref[...] = jnp.zeros_like(acc_ref)
    carry_ref[...] = jnp.zeros_like(carry_ref)

    def tile(kb, diagonal):
        start = pl.multiple_of(kb * tq, tq)
        k_blk = k_ref[0, pl.ds(start, tq), :]
        v_blk = v_ref[0, pl.ds(start, tq), :]
        z = _dot_nt(q2, k_blk)
        soft = jnp.log(1.0 + jnp.exp(-jnp.abs(z)))
        log_stay = jnp.minimum(-z, 0.0) - soft
        log_beta = jnp.minimum(z, 0.0) - soft
        if diagonal:
            row = lax.broadcasted_iota(jnp.int32, (2 * tq, tq), 0)
            col = lax.broadcasted_iota(jnp.int32, (2 * tq, tq), 1)
            keep = col < jnp.where(row >= tq, row - tq, row)
            log_stay = jnp.where(keep, log_stay, 0.0)
        after = _split_dot(log_stay, tri) + carry_ref[...]
        w = jnp.exp(log_beta + after)
        if diagonal:
            w = jnp.where(keep, w, 0.0)
        acc_ref[...] += _dot(w.astype(BF16), v_blk)
        carry_ref[...] += jnp.sum(log_stay, axis=-1, keepdims=True)

    tile(i, True)

    def body(j, c):
        tile(i - 1 - j, False)
        return c

    lax.fori_loop(0, i, body, 0)

    acc = acc_ref[...]
    o = jnp.where(lane < HEAD_DIM, acc[:tq], acc[tq:])
    o_ref[0] = _head_rms(o, ones_ref[...], g_ref[...]).astype(o_ref.dtype)


def _sb_attention(proj3, gain, tq):
    b, s, _ = proj3.shape
    n_pairs = gain.shape[1] // LANES
    tri = (jnp.arange(tq)[:, None] > jnp.arange(tq)[None, :]).astype(BF16)
    ones = _group_ones(LANES)
    return pl.pallas_call(
        functools.partial(_sb_kernel, tq=tq),
        grid=(b, n_pairs, s // tq),
        in_specs=[pl.BlockSpec((1, tq, LANES), lambda bi, p, i: (bi, i, p)),
                  pl.BlockSpec((1, s, LANES), lambda bi, p, i: (bi, 0, n_pairs + p)),
                  pl.BlockSpec((1, s, LANES), lambda bi, p, i: (bi, 0, 2 * n_pairs + p)),
                  pl.BlockSpec((tq, tq), lambda bi, p, i: (0, 0)),
                  pl.BlockSpec((LANES, LANES), lambda bi, p, i: (0, 0)),
                  pl.BlockSpec((1, LANES), lambda bi, p, i: (0, p))],
        out_specs=pl.BlockSpec((1, tq, LANES), lambda bi, p, i: (bi, i, p)),
        out_shape=jax.ShapeDtypeStruct((b, s, n_pairs * LANES), BF16),
        scratch_shapes=[pltpu.VMEM((2 * tq, LANES), F32), pltpu.VMEM((2 * tq, 1), F32)],
        compiler_params=_params(("parallel", "parallel", "arbitrary")),
        name="sb_attn",
    )(proj3, proj3, proj3, tri, ones, gain)


def _group_ones(width):
    g = jnp.arange(width) // HEAD_DIM
    return (g[:, None] == g[None, :]).astype(BF16)


def _dil_kernel(slopes_ref, q_ref, kp_ref, kc_ref, vp_ref, vc_ref, o_ref, lse_ref, *, tq, dil):
    p = pl.program_id(1)
    i = pl.program_id(3)
    sub = WINDOW_KEYS
    lane = lax.broadcasted_iota(jnp.int32, (sub, LANES), 1)
    kk = jnp.concatenate([kp_ref[0], kc_ref[0]], axis=0)
    vv = jnp.concatenate([vp_ref[0], vc_ref[0]], axis=0)

    row = lax.broadcasted_iota(jnp.int32, (2 * sub, 2 * sub), 0)
    col = lax.broadcasted_iota(jnp.int32, (2 * sub, 2 * sub), 1)
    r = jnp.where(row >= sub, row - sub, row)
    dist = r + sub - col
    in_window = (dist >= 0) & (dist <= WINDOW_KEYS)
    slope = jnp.where(row >= sub, slopes_ref[PAIR * p + 1], slopes_ref[PAIR * p])
    bias = jnp.where(in_window, -slope * (dist * dil).astype(F32), NEG_BIG)

    for j in range(tq // sub):
        q = q_ref[0, j * sub:(j + 1) * sub, :] * jnp.asarray(HEAD_DIM ** -0.5, BF16)
        zero = jnp.zeros_like(q)
        q2 = jnp.concatenate([jnp.where(lane < HEAD_DIM, q, zero),
                              jnp.where(lane < HEAD_DIM, zero, q)], axis=0)
        sc = _dot_nt(q2, kk[j * sub:(j + 2) * sub]) + bias
        if j == 0:
            sc = jnp.where(col + (i * tq - sub) >= 0, sc, NEG_BIG)
        m = jnp.max(sc, axis=-1, keepdims=True)
        e = jnp.exp(sc - m)
        l = jnp.sum(e, axis=-1, keepdims=True)
        o = _dot(e.astype(BF16), vv[j * sub:(j + 2) * sub]) / l
        lse = m + jnp.log(l)
        o_ref[0, j * sub:(j + 1) * sub, :] = jnp.where(lane < HEAD_DIM, o[:sub], o[sub:])
        lse_ref[0, j * sub:(j + 1) * sub, :] = jnp.where(
            lane < HEAD_DIM, jnp.broadcast_to(lse[:sub], (sub, LANES)),
            jnp.broadcast_to(lse[sub:], (sub, LANES)))


def _dil_branch(proj3, slopes, dil, tq, col0, n_pairs):
    b, s, width = proj3.shape
    sd = s // dil
    tiles = width // LANES
    view = proj3.reshape(b, sd, dil * width)
    ratio = tq // WINDOW_KEYS

    def cur(c):
        return lambda bi, p, r, i, sl: (bi, i, r * tiles + c + p)

    def prev(c):
        return lambda bi, p, r, i, sl: (bi, jnp.maximum(i * ratio - 1, 0), r * tiles + c + p)

    out_map = lambda bi, p, r, i, sl: (bi, i, r * n_pairs + p)
    out_sds = jax.ShapeDtypeStruct((b, sd, dil * n_pairs * LANES), F32)
    o, lse = pl.pallas_call(
        functools.partial(_dil_kernel, tq=tq, dil=dil),
        grid_spec=pltpu.PrefetchScalarGridSpec(
            num_scalar_prefetch=1,
            grid=(b, n_pairs, dil, sd // tq),
            in_specs=[pl.BlockSpec((1, tq, LANES), cur(col0)),
                      pl.BlockSpec((1, WINDOW_KEYS, LANES), prev(col0 + n_pairs)),
                      pl.BlockSpec((1, tq, LANES), cur(col0 + n_pairs)),
                      pl.BlockSpec((1, WINDOW_KEYS, LANES), prev(col0 + 2 * n_pairs)),
                      pl.BlockSpec((1, tq, LANES), cur(col0 + 2 * n_pairs))],
            out_specs=[pl.BlockSpec((1, tq, LANES), out_map),
                       pl.BlockSpec((1, tq, LANES), out_map)]),
        out_shape=[out_sds, out_sds],
        compiler_params=_params(("parallel", "parallel", "parallel", "arbitrary")),
        name=f"dil_attn_d{dil}",
    )(slopes, view, view, view, view, view)
    return o.reshape(b, s, n_pairs * LANES), lse.reshape(b, s, n_pairs * LANES)


def _mix_ln_kernel(x_ref, ma_ref, o1_ref, o2_ref, o3_ref, l1_ref, l2_ref, l3_ref, ones_ref,
                   gd_ref, w_ref, g_ref, b_ref, h_ref, *, alpha):
    l1, l2, l3 = l1_ref[...], l2_ref[...], l3_ref[...]
    m = jnp.maximum(jnp.maximum(l1, l2), l3)
    w1, w2, w3 = jnp.exp(l1 - m), jnp.exp(l2 - m), jnp.exp(l3 - m)
    o = (w1 * o1_ref[...] + w2 * o2_ref[...] + w3 * o3_ref[...]) / (w1 + w2 + w3)
    mb = _head_rms(o, ones_ref[...], gd_ref[...]).astype(BF16)
    half = ma_ref.shape[1]
    mix = _dot(ma_ref[...], w_ref[:half, :]) + _dot(mb, w_ref[half:, :])
    h_ref[...] = _layer_norm(alpha * x_ref[...] + mix, g_ref[...], b_ref[...])


def _mix_ln(x2d, ma, outs, lses, g_dil, w_out, ln_g, ln_b, alpha, tm):
    n, d = x2d.shape
    half = ma.shape[1]
    row = lambda w: pl.BlockSpec((tm, w), lambda i: (i, 0))
    full = lambda a: pl.BlockSpec(a.shape, lambda i: (0, 0))
    ones = _group_ones(half)
    return pl.pallas_call(
        functools.partial(_mix_ln_kernel, alpha=alpha),
        grid=(n // tm,),
        in_specs=[row(d), row(half)] + [row(half)] * 6
                 + [full(ones), full(g_dil), full(w_out), full(ln_g), full(ln_b)],
        out_specs=row(d),
        out_shape=jax.ShapeDtypeStruct((n, d), F32),
        compiler_params=_params(("parallel",)),
        name="mix_ln",
    )(x2d, ma, *outs, *lses, ones, g_dil, w_out, ln_g, ln_b)


def _ffn_ln_kernel(h_ref, wg_ref, wu_ref, wd_ref, g_ref, b_ref, o_ref, act_ref, *, alpha, chunk):
    h = h_ref[...]
    hb = h.astype(BF16)
    for c in range(act_ref.shape[1] // chunk):
        cols = slice(c * chunk, (c + 1) * chunk)
        gate = _dot(hb, wg_ref[:, cols])
        up = _dot(hb, wu_ref[:, cols])
        act_ref[:, cols] = (gate / (1.0 + jnp.exp(-gate)) * up).astype(BF16)
    ffn = _dot(act_ref[...], wd_ref[...])
    o_ref[...] = _layer_norm(alpha * h + ffn, g_ref[...], b_ref[...])


def _ffn_ln(h, w_gate, w_up, w_down, ln_g, ln_b, alpha, tm, chunk):
    n, d = h.shape
    d_ff = w_gate.shape[1]
    row = pl.BlockSpec((tm, d), lambda i: (i, 0))
    const = lambda a: pl.BlockSpec(a.shape, lambda i: (0, 0), pipeline_mode=pl.Buffered(1))
    return pl.pallas_call(
        functools.partial(_ffn_ln_kernel, alpha=alpha, chunk=chunk),
        grid=(n // tm,),
        in_specs=[row, const(w_gate), const(w_up), const(w_down), const(ln_g), const(ln_b)],
        out_specs=row,
        out_shape=jax.ShapeDtypeStruct((n, d), F32),
        scratch_shapes=[pltpu.VMEM((tm, d_ff), BF16)],
        compiler_params=_params(("parallel",)),
        name="ffn_ln",
    )(h, w_gate, w_up, w_down, ln_g, ln_b)


def kernel(x, w_in, g_sb, g_dil, w_out, ln1_g, ln1_b, w_gate, w_up, w_down, ln2_g, ln2_b):
    b, s, d = x.shape
    depth = w_in.shape[0]
    sb_width = g_sb.shape[1]
    dil_width = g_dil.shape[1]
    n_dil_heads = dil_width // HEAD_DIM
    alpha = (2.0 * depth) ** 0.25
    slopes = jnp.exp2(-8.0 * (jnp.arange(n_dil_heads, dtype=F32) + 1.0) / n_dil_heads)

    h = x.reshape(b * s, d)
    for l in range(depth):
        proj = _in_proj(h, w_in[l].astype(BF16), tm=512).reshape(b, s, -1)
        mixed_a = _sb_attention(proj, g_sb[l][None, :], tq=256)
        outs, lses = [], []
        for _, dil in DILATED_PATTERNS:
            o, lse = _dil_branch(proj, slopes, dil, tq=256, col0=3 * sb_width // LANES,
                                 n_pairs=dil_width // LANES)
            outs.append(o.reshape(b * s, dil_width))
            lses.append(lse.reshape(b * s, dil_width))
        h = _mix_ln(h, mixed_a.reshape(b * s, sb_width), outs, lses, g_dil[l][None, :],
                    w_out[l].astype(BF16), ln1_g[l][None, :], ln1_b[l][None, :], alpha, tm=256)
        h = _ffn_ln(h, w_gate[l].astype(BF16), w_up[l].astype(BF16), w_down[l].astype(BF16),
                    ln2_g[l][None, :], ln2_b[l][None, :], alpha, tm=512, chunk=256)
    return h.reshape(b, s, d)
```

```python
import functools
import math

import jax
import jax.numpy as jnp
from jax import lax
from jax.experimental import pallas as pl
from jax.experimental.pallas import tpu as pltpu

HEAD_DIM = 64
LANES = 128
PAIR = LANES // HEAD_DIM
DILATED_PATTERNS = ((128, 1), (512, 4), (2048, 16))
WINDOW_KEYS = 128
LN_EPS = 1e-5
RMS_EPS = 1e-6
NEG_BIG = -1e30
F32_EXP_UNDERFLOW = -104.0
VMEM_LIMIT = 48 * 1024 * 1024

F32 = jnp.float32
BF16 = jnp.bfloat16


def _dot(a, b):
    return jnp.dot(a, b, preferred_element_type=F32)


def _dot_nt(a, b):
    return lax.dot_general(a, b, (((1,), (1,)), ((), ())), preferred_element_type=F32)


def _split_dot(x, m):
    hi = x.astype(BF16)
    lo = (x - hi.astype(F32)).astype(BF16)
    return _dot(hi, m) + _dot(lo, m)


def _layer_norm(y, g, b):
    mu = jnp.mean(y, axis=-1, keepdims=True)
    d = y - mu
    var = jnp.mean(d * d, axis=-1, keepdims=True)
    return d * lax.rsqrt(var + LN_EPS) * g + b


def _head_rms(o, group_ones, gain):
    ms = _split_dot(o * o, group_ones) * (1.0 / HEAD_DIM)
    return o * lax.rsqrt(ms + RMS_EPS) * gain


def _params(sem):
    return pltpu.CompilerParams(dimension_semantics=sem, vmem_limit_bytes=VMEM_LIMIT)


def _in_proj_kernel(x_ref, w_ref, o_ref):
    o_ref[...] = _dot(x_ref[...].astype(BF16), w_ref[...]).astype(o_ref.dtype)


def _in_proj(x2d, w, tm):
    n, d = x2d.shape
    width = w.shape[1]
    return pl.pallas_call(
        _in_proj_kernel,
        grid=(n // tm,),
        in_specs=[pl.BlockSpec((tm, d), lambda i: (i, 0)),
                  pl.BlockSpec((d, width), lambda i: (0, 0))],
        out_specs=pl.BlockSpec((tm, width), lambda i: (i, 0)),
        out_shape=jax.ShapeDtypeStruct((n, width), BF16),
        compiler_params=_params(("parallel",)),
        name="in_proj",
    )(x2d, w)


def _sb_kernel(q_ref, k_ref, v_ref, tri_ref, ones_ref, g_ref, o_ref, acc_ref, carry_ref, *, tq):
    i = pl.program_id(2)
    lane = lax.broadcasted_iota(jnp.int32, (tq, LANES), 1)
    q = q_ref[0] * jnp.asarray(HEAD_DIM ** -0.5, BF16)
    zero = jnp.zeros_like(q)
    q2 = jnp.concatenate([jnp.where(lane < HEAD_DIM, q, zero),
                          jnp.where(lane < HEAD_DIM, zero, q)], axis=0)
    tri = tri_ref[...]

    acc_ref[...] = jnp.zeros_like(acc_ref)
    carry_ref[...] = jnp.zeros_like(carry_ref)

    def tile(kb, diagonal):
        start = pl.multiple_of(kb * tq, tq)
        k_blk = k_ref[0, pl.ds(start, tq), :]
        v_blk = v_ref[0, pl.ds(start, tq), :]
        z = _dot_nt(q2, k_blk)
        soft = jnp.log(1.0 + jnp.exp(-jnp.abs(z)))
        log_stay = jnp.minimum(-z, 0.0) - soft
        log_beta = jnp.minimum(z, 0.0) - soft
        if diagonal:
            row = lax.broadcasted_iota(jnp.int32, (2 * tq, tq), 0)
            col = lax.broadcasted_iota(jnp.int32, (2 * tq, tq), 1)
            keep = col < jnp.where(row >= tq, row - tq, row)
            log_stay = jnp.where(keep, log_stay, 0.0)
        after = _split_dot(log_stay, tri) + carry_ref[...]
        w = jnp.exp(log_beta + after)
        if diagonal:
            w = jnp.where(keep, w, 0.0)
        acc_ref[...] += _dot(w.astype(BF16), v_blk)
        carry_ref[...] += jnp.sum(log_stay, axis=-1, keepdims=True)

    tile(i, True)

    def more(state):
        kb, top = state
        return jnp.logical_and(kb >= 0, top > F32_EXP_UNDERFLOW)

    def body(state):
        kb, _ = state
        tile(kb, False)
        return kb - 1, jnp.max(carry_ref[...])

    lax.while_loop(more, body, (i - 1, jnp.zeros((), F32)))

    acc = acc_ref[...]
    o = jnp.where(lane < HEAD_DIM, acc[:tq], acc[tq:])
    o_ref[0] = _head_rms(o, ones_ref[...], g_ref[...]).astype(o_ref.dtype)


def _sb_attention(proj3, gain, tq):
    b, s, _ = proj3.shape
    n_pairs = gain.shape[1] // LANES
    tri = (jnp.arange(tq)[:, None] > jnp.arange(tq)[None, :]).astype(BF16)
    ones = _group_ones(LANES)
    return pl.pallas_call(
        functools.partial(_sb_kernel, tq=tq),
        grid=(b, n_pairs, s // tq),
        in_specs=[pl.BlockSpec((1, tq, LANES), lambda bi, p, i: (bi, i, p)),
                  pl.BlockSpec((1, s, LANES), lambda bi, p, i: (bi, 0, n_pairs + p)),
                  pl.BlockSpec((1, s, LANES), lambda bi, p, i: (bi, 0, 2 * n_pairs + p)),
                  pl.BlockSpec((tq, tq), lambda bi, p, i: (0, 0)),
                  pl.BlockSpec((LANES, LANES), lambda bi, p, i: (0, 0)),
                  pl.BlockSpec((1, LANES), lambda bi, p, i: (0, p))],
        out_specs=pl.BlockSpec((1, tq, LANES), lambda bi, p, i: (bi, i, p)),
        out_shape=jax.ShapeDtypeStruct((b, s, n_pairs * LANES), BF16),
        scratch_shapes=[pltpu.VMEM((2 * tq, LANES), F32), pltpu.VMEM((2 * tq, 1), F32)],
        compiler_params=_params(("parallel", "parallel", "arbitrary")),
        name="sb_attn",
    )(proj3, proj3, proj3, tri, ones, gain)


def _group_ones(width):
    g = jnp.arange(width) // HEAD_DIM
    return (g[:, None] == g[None, :]).astype(BF16)


def _dil_kernel(slopes_ref, q_ref, kp_ref, kc_ref, vp_ref, vc_ref, o_ref, lse_ref, *, tq, dil):
    p = pl.program_id(1)
    i = pl.program_id(3)
    sub = WINDOW_KEYS
    lane = lax.broadcasted_iota(jnp.int32, (sub, LANES), 1)
    kk = jnp.concatenate([kp_ref[0], kc_ref[0]], axis=0)
    vv = jnp.concatenate([vp_ref[0], vc_ref[0]], axis=0)

    row = lax.broadcasted_iota(jnp.int32, (2 * sub, 2 * sub), 0)
    col = lax.broadcasted_iota(jnp.int32, (2 * sub, 2 * sub), 1)
    r = jnp.where(row >= sub, row - sub, row)
    dist = r + sub - col
    in_window = (dist >= 0) & (dist <= WINDOW_KEYS)
    slope = jnp.where(row >= sub, slopes_ref[PAIR * p + 1], slopes_ref[PAIR * p])
    bias = jnp.where(in_window, -slope * (dist * dil).astype(F32), NEG_BIG)

    for j in range(tq // sub):
        q = q_ref[0, j * sub:(j + 1) * sub, :] * jnp.asarray(HEAD_DIM ** -0.5, BF16)
        zero = jnp.zeros_like(q)
        q2 = jnp.concatenate([jnp.where(lane < HEAD_DIM, q, zero),
                              jnp.where(lane < HEAD_DIM, zero, q)], axis=0)
        sc = _dot_nt(q2, kk[j * sub:(j + 2) * sub]) + bias
        if j == 0:
            sc = jnp.where(col + (i * tq - sub) >= 0, sc, NEG_BIG)
        m = jnp.max(sc, axis=-1, keepdims=True)
        e = jnp.exp(sc - m)
        l = jnp.sum(e, axis=-1, keepdims=True)
        o = _dot(e.astype(BF16), vv[j * sub:(j + 2) * sub]) / l
        lse = m + jnp.log(l)
        o_ref[0, j * sub:(j + 1) * sub, :] = jnp.where(lane < HEAD_DIM, o[:sub], o[sub:])
        lse_ref[0, j * sub:(j + 1) * sub, :] = jnp.where(
            lane < HEAD_DIM, jnp.broadcast_to(lse[:sub], (sub, LANES)),
            jnp.broadcast_to(lse[sub:], (sub, LANES)))


def _dil_branch(proj3, slopes, dil, tq, col0, n_pairs):
    b, s, width = proj3.shape
    sd = s // dil
    tiles = width // LANES
    view = proj3.reshape(b, sd, dil * width)
    ratio = tq // WINDOW_KEYS

    def cur(c):
        return lambda bi, p, r, i, sl: (bi, i, r * tiles + c + p)

    def prev(c):
        return lambda bi, p, r, i, sl: (bi, jnp.maximum(i * ratio - 1, 0), r * tiles + c + p)

    out_map = lambda bi, p, r, i, sl: (bi, i, r * n_pairs + p)
    out_sds = jax.ShapeDtypeStruct((b, sd, dil * n_pairs * LANES), F32)
    o, lse = pl.pallas_call(
        functools.partial(_dil_kernel, tq=tq, dil=dil),
        grid_spec=pltpu.PrefetchScalarGridSpec(
            num_scalar_prefetch=1,
            grid=(b, n_pairs, dil, sd // tq),
            in_specs=[pl.BlockSpec((1, tq, LANES), cur(col0)),
                      pl.BlockSpec((1, WINDOW_KEYS, LANES), prev(col0 + n_pairs)),
                      pl.BlockSpec((1, tq, LANES), cur(col0 + n_pairs)),
                      pl.BlockSpec((1, WINDOW_KEYS, LANES), prev(col0 + 2 * n_pairs)),
                      pl.BlockSpec((1, tq, LANES), cur(col0 + 2 * n_pairs))],
            out_specs=[pl.BlockSpec((1, tq, LANES), out_map),
                       pl.BlockSpec((1, tq, LANES), out_map)]),
        out_shape=[out_sds, out_sds],
        compiler_params=_params(("parallel", "parallel", "parallel", "arbitrary")),
        name=f"dil_attn_d{dil}",
    )(slopes, view, view, view, view, view)
    return o.reshape(b, s, n_pairs * LANES), lse.reshape(b, s, n_pairs * LANES)


def _mix_ln_kernel(x_ref, ma_ref, o1_ref, o2_ref, o3_ref, l1_ref, l2_ref, l3_ref, ones_ref,
                   gd_ref, w_ref, g_ref, b_ref, h_ref, *, alpha):
    l1, l2, l3 = l1_ref[...], l2_ref[...], l3_ref[...]
    m = jnp.maximum(jnp.maximum(l1, l2), l3)
    w1, w2, w3 = jnp.exp(l1 - m), jnp.exp(l2 - m), jnp.exp(l3 - m)
    o = (w1 * o1_ref[...] + w2 * o2_ref[...] + w3 * o3_ref[...]) / (w1 + w2 + w3)
    mb = _head_rms(o, ones_ref[...], gd_ref[...]).astype(BF16)
    half = ma_ref.shape[1]
    mix = _dot(ma_ref[...], w_ref[:half, :]) + _dot(mb, w_ref[half:, :])
    h_ref[...] = _layer_norm(alpha * x_ref[...] + mix, g_ref[...], b_ref[...])


def _mix_ln(x2d, ma, outs, lses, g_dil, w_out, ln_g, ln_b, alpha, tm):
    n, d = x2d.shape
    half = ma.shape[1]
    row = lambda w: pl.BlockSpec((tm, w), lambda i: (i, 0))
    full = lambda a: pl.BlockSpec(a.shape, lambda i: (0, 0))
    ones = _group_ones(half)
    return pl.pallas_call(
        functools.partial(_mix_ln_kernel, alpha=alpha),
        grid=(n // tm,),
        in_specs=[row(d), row(half)] + [row(half)] * 6
                 + [full(ones), full(g_dil), full(w_out), full(ln_g), full(ln_b)],
        out_specs=row(d),
        out_shape=jax.ShapeDtypeStruct((n, d), F32),
        compiler_params=_params(("parallel",)),
        name="mix_ln",
    )(x2d, ma, *outs, *lses, ones, g_dil, w_out, ln_g, ln_b)


def _ffn_ln_kernel(h_ref, wg_ref, wu_ref, wd_ref, g_ref, b_ref, o_ref, act_ref, *, alpha, chunk):
    h = h_ref[...]
    hb = h.astype(BF16)
    for c in range(act_ref.shape[1] // chunk):
        cols = slice(c * chunk, (c + 1) * chunk)
        gate = _dot(hb, wg_ref[:, cols])
        up = _dot(hb, wu_ref[:, cols])
        act_ref[:, cols] = (gate / (1.0 + jnp.exp(-gate)) * up).astype(BF16)
    ffn = _dot(act_ref[...], wd_ref[...])
    o_ref[...] = _layer_norm(alpha * h + ffn, g_ref[...], b_ref[...])


def _ffn_ln(h, w_gate, w_up, w_down, ln_g, ln_b, alpha, tm, chunk):
    n, d = h.shape
    d_ff = w_gate.shape[1]
    row = pl.BlockSpec((tm, d), lambda i: (i, 0))
    const = lambda a: pl.BlockSpec(a.shape, lambda i: (0, 0), pipeline_mode=pl.Buffered(1))
    return pl.pallas_call(
        functools.partial(_ffn_ln_kernel, alpha=alpha, chunk=chunk),
        grid=(n // tm,),
        in_specs=[row, const(w_gate), const(w_up), const(w_down), const(ln_g), const(ln_b)],
        out_specs=row,
        out_shape=jax.ShapeDtypeStruct((n, d), F32),
        scratch_shapes=[pltpu.VMEM((tm, d_ff), BF16)],
        compiler_params=_params(("parallel",)),
        name="ffn_ln",
    )(h, w_gate, w_up, w_down, ln_g, ln_b)


def kernel(x, w_in, g_sb, g_dil, w_out, ln1_g, ln1_b, w_gate, w_up, w_down, ln2_g, ln2_b):
    b, s, d = x.shape
    depth = w_in.shape[0]
    sb_width = g_sb.shape[1]
    dil_width = g_dil.shape[1]
    n_dil_heads = dil_width // HEAD_DIM
    alpha = (2.0 * depth) ** 0.25
    slopes = jnp.exp2(-8.0 * (jnp.arange(n_dil_heads, dtype=F32) + 1.0) / n_dil_heads)

    h = x.reshape(b * s, d)
    for l in range(depth):
        proj = _in_proj(h, w_in[l].astype(BF16), tm=512).reshape(b, s, -1)
        mixed_a = _sb_attention(proj, g_sb[l][None, :], tq=256)
        outs, lses = [], []
        for _, dil in DILATED_PATTERNS:
            o, lse = _dil_branch(proj, slopes, dil, tq=256, col0=3 * sb_width // LANES,
                                 n_pairs=dil_width // LANES)
            outs.append(o.reshape(b * s, dil_width))
            lses.append(lse.reshape(b * s, dil_width))
        h = _mix_ln(h, mixed_a.reshape(b * s, sb_width), outs, lses, g_dil[l][None, :],
                    w_out[l].astype(BF16), ln1_g[l][None, :], ln1_b[l][None, :], alpha, tm=256)
        h = _ffn_ln(h, w_gate[l].astype(BF16), w_up[l].astype(BF16), w_down[l].astype(BF16),
                    ln2_g[l][None, :], ln2_b[l][None, :], alpha, tm=512, chunk=256)
    return h.reshape(b, s, d)
```

```python
import functools

import jax
import jax.numpy as jnp
from jax import lax
from jax.experimental import pallas as pl
from jax.experimental.pallas import tpu as pltpu

HEAD_DIM = 64
LANES = 128
PAIR = LANES // HEAD_DIM
DILATIONS = (1, 4, 16)
WINDOW_KEYS = 128
SUB = WINDOW_KEYS
SUPER = SUB * max(DILATIONS)
GROUP = 4
LN_EPS = 1e-5
RMS_EPS = 1e-6
NEG_BIG = -1e30
F32_EXP_UNDERFLOW = -104.0
VMEM_LIMIT = 48 * 1024 * 1024

F32 = jnp.float32
BF16 = jnp.bfloat16


def _dot(a, b):
    return jnp.dot(a, b, preferred_element_type=F32)


def _dot_nt(a, b):
    return lax.dot_general(a, b, (((1,), (1,)), ((), ())), preferred_element_type=F32)


def _split_dot(x, m):
    hi = x.astype(BF16)
    lo = (x - hi.astype(F32)).astype(BF16)
    return _dot(hi, m) + _dot(lo, m)


def _layer_norm(y, g, b):
    mu = jnp.mean(y, axis=-1, keepdims=True)
    d = y - mu
    var = jnp.mean(d * d, axis=-1, keepdims=True)
    return d * lax.rsqrt(var + LN_EPS) * g + b


def _head_rms(o, group_ones, gain):
    ms = _split_dot(o * o, group_ones) * (1.0 / HEAD_DIM)
    return o * lax.rsqrt(ms + RMS_EPS) * gain


def _stack_heads(q):
    lane = lax.broadcasted_iota(jnp.int32, q.shape, 1)
    q = q * jnp.asarray(HEAD_DIM ** -0.5, BF16)
    zero = jnp.zeros_like(q)
    return jnp.concatenate([jnp.where(lane < HEAD_DIM, q, zero),
                            jnp.where(lane < HEAD_DIM, zero, q)], axis=0)


def _unstack_heads(x):
    rows = x.shape[0] // PAIR
    lane = lax.broadcasted_iota(jnp.int32, (rows, LANES), 1)
    return jnp.where(lane < HEAD_DIM, jnp.broadcast_to(x[:rows], (rows, LANES)),
                     jnp.broadcast_to(x[rows:], (rows, LANES)))


def _group_ones(width):
    g = jnp.arange(width) // HEAD_DIM
    return (g[:, None] == g[None, :]).astype(BF16)


def _params(sem):
    return pltpu.CompilerParams(dimension_semantics=sem, vmem_limit_bytes=VMEM_LIMIT)


def _in_proj_kernel(x_ref, w_ref, a_ref, b_ref):
    proj = _dot(x_ref[...].astype(BF16), w_ref[...])
    wa = a_ref.shape[1]
    a_ref[...] = proj[:, :wa].astype(a_ref.dtype)
    b_ref[...] = proj[:, wa:]


def _in_proj(x2d, w, wa, tm):
    n, d = x2d.shape
    wb = w.shape[1] - wa
    return pl.pallas_call(
        _in_proj_kernel,
        grid=(n // tm,),
        in_specs=[pl.BlockSpec((tm, d), lambda i: (i, 0)),
                  pl.BlockSpec(w.shape, lambda i: (0, 0))],
        out_specs=[pl.BlockSpec((tm, wa), lambda i: (i, 0)),
                   pl.BlockSpec((tm, wb), lambda i: (i, 0))],
        out_shape=[jax.ShapeDtypeStruct((n, wa), BF16), jax.ShapeDtypeStruct((n, wb), F32)],
        compiler_params=_params(("parallel",)),
        name="in_proj",
    )(x2d, w)


def _sb_kernel(q_ref, k_ref, v_ref, tri_ref, ones_ref, g_ref, o_ref, acc_ref, carry_ref, *, tq):
    i = pl.program_id(2)
    q2 = _stack_heads(q_ref[0])
    tri = tri_ref[...]

    acc_ref[...] = jnp.zeros_like(acc_ref)
    carry_ref[...] = jnp.zeros_like(carry_ref)

    def tile(kb, diagonal):
        start = pl.multiple_of(kb * tq, tq)
        k_blk = k_ref[0, pl.ds(start, tq), :]
        v_blk = v_ref[0, pl.ds(start, tq), :]
        z = _dot_nt(q2, k_blk)
        soft = jnp.log(1.0 + jnp.exp(-jnp.abs(z)))
        log_stay = jnp.minimum(-z, 0.0) - soft
        log_beta = jnp.minimum(z, 0.0) - soft
        if diagonal:
            row = lax.broadcasted_iota(jnp.int32, (2 * tq, tq), 0)
            col = lax.broadcasted_iota(jnp.int32, (2 * tq, tq), 1)
            keep = col < jnp.where(row >= tq, row - tq, row)
            log_stay = jnp.where(keep, log_stay, 0.0)
        after = _split_dot(log_stay, tri) + carry_ref[...]
        w = jnp.exp(log_beta + after)
        if diagonal:
            w = jnp.where(keep, w, 0.0)
        acc_ref[...] += _dot(w.astype(BF16), v_blk)
        carry_ref[...] += jnp.sum(log_stay, axis=-1, keepdims=True)

    tile(i, True)

    def more(state):
        kb, top = state
        return jnp.logical_and(kb >= 0, top > F32_EXP_UNDERFLOW)

    def body(state):
        kb, _ = state
        tile(kb, False)
        return kb - 1, jnp.max(carry_ref[...])

    lax.while_loop(more, body, (i - 1, jnp.zeros((), F32)))

    o = _unstack_heads(acc_ref[...])
    o_ref[0] = _head_rms(o, ones_ref[...], g_ref[...]).astype(o_ref.dtype)


def _sb_attention(qkv, gain, tq):
    b, s, _ = qkv.shape
    n_pairs = gain.shape[1] // LANES
    tri = (jnp.arange(tq)[:, None] > jnp.arange(tq)[None, :]).astype(BF16)
    ones = _group_ones(LANES)
    return pl.pallas_call(
        functools.partial(_sb_kernel, tq=tq),
        grid=(b, n_pairs, s // tq),
        in_specs=[pl.BlockSpec((1, tq, LANES), lambda bi, p, i: (bi, i, p)),
                  pl.BlockSpec((1, s, LANES), lambda bi, p, i: (bi, 0, n_pairs + p)),
                  pl.BlockSpec((1, s, LANES), lambda bi, p, i: (bi, 0, 2 * n_pairs + p)),
                  pl.BlockSpec((tq, tq), lambda bi, p, i: (0, 0)),
                  pl.BlockSpec((LANES, LANES), lambda bi, p, i: (0, 0)),
                  pl.BlockSpec((1, LANES), lambda bi, p, i: (0, p))],
        out_specs=pl.BlockSpec((1, tq, LANES), lambda bi, p, i: (bi, i, p)),
        out_shape=jax.ShapeDtypeStruct((b, s, n_pairs * LANES), BF16),
        scratch_shapes=[pltpu.VMEM((2 * tq, LANES), F32), pltpu.VMEM((2 * tq, 1), F32)],
        compiler_params=_params(("parallel", "parallel", "arbitrary")),
        name="sb_attn",
    )(qkv, qkv, qkv, tri, ones, gain)


def _dil_kernel(slopes_ref, q_ref, kp_ref, kc_ref, vp_ref, vc_ref, ones_ref, g_ref, o_ref,
                bias_ref, o4_ref, l4_ref, o16_ref, l16_ref):
    p = pl.program_id(1)
    first = pl.program_id(2) == 0

    row = lax.broadcasted_iota(jnp.int32, (2 * SUB, 2 * SUB), 0)
    col = lax.broadcasted_iota(jnp.int32, (2 * SUB, 2 * SUB), 1)
    dist = jnp.where(row >= SUB, row - SUB, row) + SUB - col
    in_window = (dist >= 0) & (dist <= WINDOW_KEYS)
    slope = jnp.where(row >= SUB, slopes_ref[PAIR * p + 1], slopes_ref[PAIR * p])
    before_start = jnp.where(col < SUB, jnp.where(first, NEG_BIG, 0.0), 0.0)
    for n, d in enumerate(DILATIONS):
        bias = jnp.where(in_window, -slope * (dist * d).astype(F32), NEG_BIG)
        bias_ref[2 * n] = bias
        bias_ref[2 * n + 1] = bias + before_start

    def rows(ref, start, d):
        if d == 1:
            return ref[0, pl.ds(start, SUB), :]
        return ref[0, pl.ds(start, SUB, stride=d), :]

    def tile(n, start, at_edge):
        d = DILATIONS[n]
        if at_edge:
            k_lo, v_lo = rows(kp_ref, start + SUPER - SUB * d, d), rows(vp_ref, start + SUPER - SUB * d, d)
        else:
            k_lo, v_lo = rows(kc_ref, start - SUB * d, d), rows(vc_ref, start - SUB * d, d)
        kk = jnp.concatenate([k_lo, rows(kc_ref, start, d)], axis=0).astype(BF16)
        vv = jnp.concatenate([v_lo, rows(vc_ref, start, d)], axis=0).astype(BF16)
        q2 = _stack_heads(rows(q_ref, start, d).astype(BF16))
        sc = _dot_nt(q2, kk) + bias_ref[2 * n + (1 if at_edge else 0)]
        m = jnp.max(sc, axis=-1, keepdims=True)
        e = jnp.exp(sc - m)
        l = jnp.sum(e, axis=-1, keepdims=True)
        o = _dot(e.astype(BF16), vv) / l
        lane = lax.broadcasted_iota(jnp.int32, (SUB, LANES), 1)
        return jnp.where(lane < HEAD_DIM, o[:SUB], o[SUB:]), _unstack_heads(m + jnp.log(l))

    def strided_tile(n, start, at_edge, o_s, l_s):
        d = DILATIONS[n]
        o, lse = tile(n, start, at_edge)
        o_s[pl.ds(start, SUB, stride=d), :] = o
        l_s[pl.ds(start, SUB, stride=d), :] = lse

    def classes16(g, c):
        for u in range(GROUP):
            strided_tile(2, g * GROUP + u, True, o16_ref, l16_ref)
        return c

    lax.fori_loop(0, DILATIONS[2] // GROUP, classes16, 0)

    for r in range(DILATIONS[1]):
        strided_tile(1, r, True, o4_ref, l4_ref)

    def classes4(j, c):
        for r in range(DILATIONS[1]):
            strided_tile(1, r + j * (SUB * DILATIONS[1]), False, o4_ref, l4_ref)
        return c

    lax.fori_loop(1, SUPER // (SUB * DILATIONS[1]), classes4, 0)

    def finish(start, at_edge):
        o1, l1 = tile(0, start, at_edge)
        span = pl.ds(start, SUB)
        l4, l16 = l4_ref[span, :], l16_ref[span, :]
        top = jnp.maximum(jnp.maximum(l1, l4), l16)
        w1, w4, w16 = jnp.exp(l1 - top), jnp.exp(l4 - top), jnp.exp(l16 - top)
        o = (w1 * o1 + w4 * o4_ref[span, :] + w16 * o16_ref[span, :]) / (w1 + w4 + w16)
        o_ref[0, span, :] = _head_rms(o, ones_ref[...], g_ref[...]).astype(o_ref.dtype)

    for u in range(GROUP):
        finish(u * SUB, u == 0)

    def blocks1(g, c):
        for u in range(GROUP):
            finish(pl.multiple_of((g * GROUP + u) * SUB, SUB), False)
        return c

    lax.fori_loop(1, SUPER // (SUB * GROUP), blocks1, 0)


def _dil_attention(qkv, slopes, gain):
    b, s, _ = qkv.shape
    n_pairs = gain.shape[1] // LANES
    ones = _group_ones(LANES)
    cur = lambda c: (lambda bi, p, i, sl: (bi, i, c * n_pairs + p))
    prev = lambda c: (lambda bi, p, i, sl: (bi, jnp.maximum(i - 1, 0), c * n_pairs + p))
    blk = (1, SUPER, LANES)
    work = pltpu.VMEM((SUPER, LANES), F32)
    return pl.pallas_call(
        _dil_kernel,
        grid_spec=pltpu.PrefetchScalarGridSpec(
            num_scalar_prefetch=1,
            grid=(b, n_pairs, s // SUPER),
            in_specs=[pl.BlockSpec(blk, cur(0)),
                      pl.BlockSpec(blk, prev(1)), pl.BlockSpec(blk, cur(1)),
                      pl.BlockSpec(blk, prev(2)), pl.BlockSpec(blk, cur(2)),
                      pl.BlockSpec((LANES, LANES), lambda bi, p, i, sl: (0, 0)),
                      pl.BlockSpec((1, LANES), lambda bi, p, i, sl: (0, p))],
            out_specs=pl.BlockSpec(blk, cur(0)),
            scratch_shapes=[pltpu.VMEM((2 * len(DILATIONS), 2 * SUB, 2 * SUB), F32),
                            work, work, work, work]),
        out_shape=jax.ShapeDtypeStruct((b, s, n_pairs * LANES), BF16),
        compiler_params=_params(("parallel", "parallel", "arbitrary")),
        name="dil_attn",
    )(slopes, qkv, qkv, qkv, qkv, qkv, ones, gain)


def _mix_ln_kernel(x_ref, ma_ref, mb_ref, w_ref, g_ref, b_ref, h_ref, *, alpha):
    half = ma_ref.shape[1]
    mix = _dot(ma_ref[...], w_ref[:half, :]) + _dot(mb_ref[...], w_ref[half:, :])
    h_ref[...] = _layer_norm(alpha * x_ref[...] + mix, g_ref[...], b_ref[...])


def _mix_ln(x2d, ma, mb, w_out, ln_g, ln_b, alpha, tm):
    n, d = x2d.shape
    row = lambda w: pl.BlockSpec((tm, w), lambda i: (i, 0))
    full = lambda a: pl.BlockSpec(a.shape, lambda i: (0, 0))
    return pl.pallas_call(
        functools.partial(_mix_ln_kernel, alpha=alpha),
        grid=(n // tm,),
        in_specs=[row(d), row(ma.shape[1]), row(mb.shape[1]), full(w_out), full(ln_g), full(ln_b)],
        out_specs=row(d),
        out_shape=jax.ShapeDtypeStruct((n, d), F32),
        compiler_params=_params(("parallel",)),
        name="mix_ln",
    )(x2d, ma, mb, w_out, ln_g, ln_b)


def _ffn_ln_kernel(h_ref, wg_ref, wu_ref, wd_ref, g_ref, b_ref, o_ref, act_ref, *, alpha, chunk):
    h = h_ref[...]
    hb = h.astype(BF16)
    for c in range(act_ref.shape[1] // chunk):
        cols = slice(c * chunk, (c + 1) * chunk)
        gate = _dot(hb, wg_ref[:, cols])
        up = _dot(hb, wu_ref[:, cols])
        act_ref[:, cols] = (gate / (1.0 + jnp.exp(-gate)) * up).astype(BF16)
    ffn = _dot(act_ref[...], wd_ref[...])
    o_ref[...] = _layer_norm(alpha * h + ffn, g_ref[...], b_ref[...])


def _ffn_ln(h, w_gate, w_up, w_down, ln_g, ln_b, alpha, tm, chunk):
    n, d = h.shape
    d_ff = w_gate.shape[1]
    row = pl.BlockSpec((tm, d), lambda i: (i, 0))
    const = lambda a: pl.BlockSpec(a.shape, lambda i: (0, 0), pipeline_mode=pl.Buffered(1))
    return pl.pallas_call(
        functools.partial(_ffn_ln_kernel, alpha=alpha, chunk=chunk),
        grid=(n // tm,),
        in_specs=[row, const(w_gate), const(w_up), const(w_down), const(ln_g), const(ln_b)],
        out_specs=row,
        out_shape=jax.ShapeDtypeStruct((n, d), F32),
        scratch_shapes=[pltpu.VMEM((tm, d_ff), BF16)],
        compiler_params=_params(("parallel",)),
        name="ffn_ln",
    )(h, w_gate, w_up, w_down, ln_g, ln_b)


def kernel(x, w_in, g_sb, g_dil, w_out, ln1_g, ln1_b, w_gate, w_up, w_down, ln2_g, ln2_b):
    b, s, d = x.shape
    depth = w_in.shape[0]
    sb_width = g_sb.shape[1]
    dil_width = g_dil.shape[1]
    n_dil_heads = dil_width // HEAD_DIM
    alpha = (2.0 * depth) ** 0.25
    slopes = jnp.exp2(-8.0 * (jnp.arange(n_dil_heads, dtype=F32) + 1.0) / n_dil_heads)
    assert s % SUPER == 0

    h = x.reshape(b * s, d)
    for l in range(depth):
        qkv_a, qkv_b = _in_proj(h, w_in[l].astype(BF16), wa=3 * sb_width, tm=512)
        mixed_a = _sb_attention(qkv_a.reshape(b, s, -1), g_sb[l][None, :], tq=256)
        mixed_b = _dil_attention(qkv_b.reshape(b, s, -1), slopes, g_dil[l][None, :])
        h = _mix_ln(h, mixed_a.reshape(b * s, sb_width), mixed_b.reshape(b * s, dil_width),
                    w_out[l].astype(BF16), ln1_g[l][None, :], ln1_b[l][None, :], alpha, tm=512)
        h = _ffn_ln(h, w_gate[l].astype(BF16), w_up[l].astype(BF16), w_down[l].astype(BF16),
                    ln2_g[l][None, :], ln2_b[l][None, :], alpha, tm=512, chunk=256)
    return h.reshape(b, s, d)
```

```python
import functools

import jax
import jax.numpy as jnp
from jax import lax
from jax.experimental import pallas as pl
from jax.experimental.pallas import tpu as pltpu

HEAD_DIM = 64
LANES = 128
PAIR = LANES // HEAD_DIM
DILATIONS = (1, 4, 16)
WINDOW_KEYS = 128
SUB = WINDOW_KEYS
SUPER = SUB * max(DILATIONS)
GROUP = 4
LN_EPS = 1e-5
RMS_EPS = 1e-6
NEG_BIG = -1e30
F32_EXP_UNDERFLOW = -104.0
LOG2_E = 1.4426950408889634
VMEM_LIMIT = 48 * 1024 * 1024

F32 = jnp.float32
BF16 = jnp.bfloat16


def _dot(a, b):
    return jnp.dot(a, b, preferred_element_type=F32)


def _dot_nt(a, b):
    return lax.dot_general(a, b, (((1,), (1,)), ((), ())), preferred_element_type=F32)


def _split_dot(x, m):
    hi = x.astype(BF16)
    lo = (x - hi.astype(F32)).astype(BF16)
    return _dot(hi, m) + _dot(lo, m)


def _layer_norm(y, g, b):
    mu = jnp.mean(y, axis=-1, keepdims=True)
    d = y - mu
    var = jnp.mean(d * d, axis=-1, keepdims=True)
    return d * lax.rsqrt(var + LN_EPS) * g + b


def _head_rms(o, group_ones, gain):
    ms = _split_dot(o * o, group_ones) * (1.0 / HEAD_DIM)
    return o * lax.rsqrt(ms + RMS_EPS) * gain


def _stack_heads(q):
    lane = lax.broadcasted_iota(jnp.int32, q.shape, 1)
    q = q * jnp.asarray(HEAD_DIM ** -0.5, BF16)
    zero = jnp.zeros_like(q)
    return jnp.concatenate([jnp.where(lane < HEAD_DIM, q, zero),
                            jnp.where(lane < HEAD_DIM, zero, q)], axis=0)


def _unstack_heads(x):
    rows = x.shape[0] // PAIR
    lane = lax.broadcasted_iota(jnp.int32, (rows, LANES), 1)
    return jnp.where(lane < HEAD_DIM, jnp.broadcast_to(x[:rows], (rows, LANES)),
                     jnp.broadcast_to(x[rows:], (rows, LANES)))


def _exp(sign, x):
    return jnp.exp2(x * (sign * LOG2_E))


def _group_ones(width):
    g = jnp.arange(width) // HEAD_DIM
    return (g[:, None] == g[None, :]).astype(BF16)


def _params(sem):
    return pltpu.CompilerParams(dimension_semantics=sem, vmem_limit_bytes=VMEM_LIMIT)


def _in_proj_kernel(x_ref, w_ref, wvt_ref, a_ref, vt_ref, b_ref):
    xb = x_ref[...].astype(BF16)
    proj = _dot(xb, w_ref[...])
    wa = a_ref.shape[1]
    a_ref[...] = proj[:, :wa].astype(a_ref.dtype)
    b_ref[...] = proj[:, wa:]
    vt_ref[...] = _dot_nt(wvt_ref[...], xb).astype(vt_ref.dtype)


def _in_proj(x2d, w, wvt, wa, tm):
    n, d = x2d.shape
    wb = w.shape[1] - wa
    wv = wvt.shape[0]
    return pl.pallas_call(
        _in_proj_kernel,
        grid=(n // tm,),
        in_specs=[pl.BlockSpec((tm, d), lambda i: (i, 0)),
                  pl.BlockSpec(w.shape, lambda i: (0, 0)),
                  pl.BlockSpec(wvt.shape, lambda i: (0, 0))],
        out_specs=[pl.BlockSpec((tm, wa), lambda i: (i, 0)),
                   pl.BlockSpec((wv, tm), lambda i: (0, i)),
                   pl.BlockSpec((tm, wb), lambda i: (i, 0))],
        out_shape=[jax.ShapeDtypeStruct((n, wa), BF16), jax.ShapeDtypeStruct((wv, n), BF16),
                   jax.ShapeDtypeStruct((n, wb), F32)],
        compiler_params=_params(("parallel",)),
        name="in_proj",
    )(x2d, w, wvt)


def _sb_kernel(q_ref, k_ref, vt_ref, ntri_ref, ones_ref, g_ref, o_ref, acc_ref, carry_ref, *, tq):
    i = pl.program_id(2)
    pairs = range(q_ref.shape[2] // LANES)
    cols = [slice(t * LANES, (t + 1) * LANES) for t in pairs]
    q2 = [_stack_heads(q_ref[0, :, cols[t]]) for t in pairs]

    acc_ref[...] = jnp.zeros_like(acc_ref)
    carry_ref[...] = jnp.zeros_like(carry_ref)

    def tile(kb, diagonal):
        start = pl.multiple_of(kb * tq, tq)
        if diagonal:
            key = lax.broadcasted_iota(jnp.int32, (tq, 2 * tq), 0)
            col = lax.broadcasted_iota(jnp.int32, (tq, 2 * tq), 1)
            keep = key < jnp.where(col >= tq, col - tq, col)
        k_blks = [k_ref[0, pl.ds(start, tq), cols[t]] for t in pairs]
        vt_blks = [vt_ref[cols[t], pl.ds(start, tq)] for t in pairs]
        ntri = ntri_ref[...]
        acc_old = [acc_ref[t] for t in pairs]
        carry_old = [carry_ref[t] for t in pairs]
        zs = [_dot_nt(k_blks[t], q2[t]) for t in pairs]
        log_betas, afters, carry_new = [], [], []
        for t in pairs:
            z = zs[t]
            stay = jnp.maximum(z, 0.0) + jnp.log(1.0 + _exp(-1.0, jnp.abs(z)))
            log_betas.append(z - stay)
            if diagonal:
                stay = jnp.where(keep, stay, 0.0)
            hi = stay.astype(BF16)
            lo = (stay - hi.astype(F32)).astype(BF16)
            afters.append(_dot(ntri, jnp.concatenate([hi, lo], axis=0)))
            carry_new.append(carry_old[t] - jnp.sum(stay, axis=0, keepdims=True))
        for t in pairs:
            w = _exp(1.0, log_betas[t] + afters[t] + carry_old[t])
            if diagonal:
                w = jnp.where(keep, w, 0.0)
            acc_ref[t] = acc_old[t] + _dot(vt_blks[t], w.astype(BF16))
            carry_ref[t] = carry_new[t]

    tile(i, True)

    def more(state):
        kb, top = state
        return jnp.logical_and(kb >= 0, top > F32_EXP_UNDERFLOW)

    def body(state):
        kb, _ = state
        tile(kb, False)
        return kb - 1, jnp.max(carry_ref[...])

    lax.while_loop(more, body, (i - 1, jnp.zeros((), F32)))

    feature = lax.broadcasted_iota(jnp.int32, (LANES, tq), 0)
    for t in pairs:
        acc = acc_ref[t]
        o = jnp.where(feature < HEAD_DIM, acc[:, :tq], acc[:, tq:]).T
        o_ref[0, :, cols[t]] = _head_rms(o, ones_ref[...], g_ref[:, cols[t]]).astype(o_ref.dtype)


def _sb_attention(qk, vt, gain, tq, width):
    b, s, _ = qk.shape
    groups = gain.shape[1] // width
    ntri = -(jnp.arange(tq)[None, :] > jnp.arange(tq)[:, None]).astype(BF16)
    ntri = jnp.concatenate([ntri, ntri], axis=1)
    ones = _group_ones(LANES)
    return pl.pallas_call(
        functools.partial(_sb_kernel, tq=tq),
        grid=(b, groups, s // tq),
        in_specs=[pl.BlockSpec((1, tq, width), lambda bi, p, i: (bi, i, p)),
                  pl.BlockSpec((1, s, width), lambda bi, p, i: (bi, 0, groups + p)),
                  pl.BlockSpec((width, s), lambda bi, p, i: (p, bi)),
                  pl.BlockSpec((tq, 2 * tq), lambda bi, p, i: (0, 0)),
                  pl.BlockSpec((LANES, LANES), lambda bi, p, i: (0, 0)),
                  pl.BlockSpec((1, width), lambda bi, p, i: (0, p))],
        out_specs=pl.BlockSpec((1, tq, width), lambda bi, p, i: (bi, i, p)),
        out_shape=jax.ShapeDtypeStruct((b, s, groups * width), BF16),
        scratch_shapes=[pltpu.VMEM((width // LANES, LANES, 2 * tq), F32),
                        pltpu.VMEM((width // LANES, 1, 2 * tq), F32)],
        compiler_params=_params(("parallel", "parallel", "arbitrary")),
        name="sb_attn",
    )(qk, qk, vt, ntri, ones, gain)


def _dil_kernel(slopes_ref, q_ref, kp_ref, kc_ref, vp_ref, vc_ref, ones_ref, g_ref, o_ref,
                bias_ref, o4_ref, l4_ref, o16_ref, l16_ref):
    p = pl.program_id(1)
    first = pl.program_id(2) == 0

    row = lax.broadcasted_iota(jnp.int32, (2 * SUB, 2 * SUB), 0)
    col = lax.broadcasted_iota(jnp.int32, (2 * SUB, 2 * SUB), 1)
    dist = jnp.where(row >= SUB, row - SUB, row) + SUB - col
    in_window = (dist >= 0) & (dist <= WINDOW_KEYS)
    slope = jnp.where(row >= SUB, slopes_ref[PAIR * p + 1], slopes_ref[PAIR * p])
    before_start = jnp.where(col < SUB, jnp.where(first, NEG_BIG, 0.0), 0.0)
    for n, d in enumerate(DILATIONS):
        bias = jnp.where(in_window, -slope * (dist * d).astype(F32), NEG_BIG)
        bias_ref[2 * n] = bias
        bias_ref[2 * n + 1] = bias + before_start

    def rows(ref, start, d):
        if d == 1:
            return ref[0, pl.ds(start, SUB), :]
        return ref[0, pl.ds(start, SUB, stride=d), :]

    def scores(n, start, at_edge):
        d = DILATIONS[n]
        if at_edge:
            k_lo, v_lo = rows(kp_ref, start + SUPER - SUB * d, d), rows(vp_ref, start + SUPER - SUB * d, d)
        else:
            k_lo, v_lo = rows(kc_ref, start - SUB * d, d), rows(vc_ref, start - SUB * d, d)
        kk = jnp.concatenate([k_lo, rows(kc_ref, start, d)], axis=0).astype(BF16)
        vv = jnp.concatenate([v_lo, rows(vc_ref, start, d)], axis=0).astype(BF16)
        q2 = _stack_heads(rows(q_ref, start, d).astype(BF16))
        return _dot_nt(q2, kk), vv

    def softmax_pv(n, at_edge, qk, vv):
        sc = qk + bias_ref[2 * n + (1 if at_edge else 0)]
        m = jnp.max(sc, axis=-1, keepdims=True)
        e = jnp.exp(sc - m)
        l = jnp.sum(e, axis=-1, keepdims=True)
        o = _dot(e.astype(BF16), vv) / l
        lane = lax.broadcasted_iota(jnp.int32, (SUB, LANES), 1)
        return jnp.where(lane < HEAD_DIM, o[:SUB], o[SUB:]), _unstack_heads(m + jnp.log(l))

    def tiles(n, starts, edges):
        staged = [scores(n, start, edge) for start, edge in zip(starts, edges)]
        return [softmax_pv(n, edge, qk, vv) for (qk, vv), edge in zip(staged, edges)]

    def strided_tiles(n, starts, at_edge, o_s, l_s):
        for start, (o, lse) in zip(starts, tiles(n, starts, [at_edge] * GROUP)):
            o_s[pl.ds(start, SUB, stride=DILATIONS[n]), :] = o
            l_s[pl.ds(start, SUB, stride=DILATIONS[n]), :] = lse

    def classes16(g, c):
        strided_tiles(2, [g * GROUP + u for u in range(GROUP)], True, o16_ref, l16_ref)
        return c

    lax.fori_loop(0, DILATIONS[2] // GROUP, classes16, 0)

    strided_tiles(1, list(range(DILATIONS[1])), True, o4_ref, l4_ref)

    def classes4(j, c):
        base = j * (SUB * DILATIONS[1])
        strided_tiles(1, [base + r for r in range(DILATIONS[1])], False, o4_ref, l4_ref)
        return c

    lax.fori_loop(1, SUPER // (SUB * DILATIONS[1]), classes4, 0)

    def finish(starts, edges):
        for start, (o1, l1) in zip(starts, tiles(0, starts, edges)):
            span = pl.ds(start, SUB)
            l4, l16 = l4_ref[span, :], l16_ref[span, :]
            top = jnp.maximum(jnp.maximum(l1, l4), l16)
            w1, w4, w16 = jnp.exp(l1 - top), jnp.exp(l4 - top), jnp.exp(l16 - top)
            o = (w1 * o1 + w4 * o4_ref[span, :] + w16 * o16_ref[span, :]) / (w1 + w4 + w16)
            o_ref[0, span, :] = _head_rms(o, ones_ref[...], g_ref[...]).astype(o_ref.dtype)

    finish([u * SUB for u in range(GROUP)], [u == 0 for u in range(GROUP)])

    def blocks1(g, c):
        first = pl.multiple_of(g * (GROUP * SUB), SUB)
        finish([first + u * SUB for u in range(GROUP)], [False] * GROUP)
        return c

    lax.fori_loop(1, SUPER // (SUB * GROUP), blocks1, 0)


def _dil_attention(qkv, slopes, gain):
    b, s, _ = qkv.shape
    n_pairs = gain.shape[1] // LANES
    ones = _group_ones(LANES)
    cur = lambda c: (lambda bi, p, i, sl: (bi, i, c * n_pairs + p))
    prev = lambda c: (lambda bi, p, i, sl: (bi, jnp.maximum(i - 1, 0), c * n_pairs + p))
    blk = (1, SUPER, LANES)
    work = pltpu.VMEM((SUPER, LANES), F32)
    return pl.pallas_call(
        _dil_kernel,
        grid_spec=pltpu.PrefetchScalarGridSpec(
            num_scalar_prefetch=1,
            grid=(b, n_pairs, s // SUPER),
            in_specs=[pl.BlockSpec(blk, cur(0)),
                      pl.BlockSpec(blk, prev(1)), pl.BlockSpec(blk, cur(1)),
                      pl.BlockSpec(blk, prev(2)), pl.BlockSpec(blk, cur(2)),
                      pl.BlockSpec((LANES, LANES), lambda bi, p, i, sl: (0, 0)),
                      pl.BlockSpec((1, LANES), lambda bi, p, i, sl: (0, p))],
            out_specs=pl.BlockSpec(blk, cur(0)),
            scratch_shapes=[pltpu.VMEM((2 * len(DILATIONS), 2 * SUB, 2 * SUB), F32),
                            work, work, work, work]),
        out_shape=jax.ShapeDtypeStruct((b, s, n_pairs * LANES), BF16),
        compiler_params=_params(("parallel", "parallel", "arbitrary")),
        name="dil_attn",
    )(slopes, qkv, qkv, qkv, qkv, qkv, ones, gain)


def _mix_ln_kernel(x_ref, ma_ref, mb_ref, w_ref, g_ref, b_ref, h_ref, *, alpha):
    half = ma_ref.shape[1]
    mix = _dot(ma_ref[...], w_ref[:half, :]) + _dot(mb_ref[...], w_ref[half:, :])
    h_ref[...] = _layer_norm(alpha * x_ref[...] + mix, g_ref[...], b_ref[...])


def _mix_ln(x2d, ma, mb, w_out, ln_g, ln_b, alpha, tm):
    n, d = x2d.shape
    row = lambda w: pl.BlockSpec((tm, w), lambda i: (i, 0))
    full = lambda a: pl.BlockSpec(a.shape, lambda i: (0, 0))
    return pl.pallas_call(
        functools.partial(_mix_ln_kernel, alpha=alpha),
        grid=(n // tm,),
        in_specs=[row(d), row(ma.shape[1]), row(mb.shape[1]), full(w_out), full(ln_g), full(ln_b)],
        out_specs=row(d),
        out_shape=jax.ShapeDtypeStruct((n, d), F32),
        compiler_params=_params(("parallel",)),
        name="mix_ln",
    )(x2d, ma, mb, w_out, ln_g, ln_b)


def _ffn_ln_kernel(h_ref, wg_ref, wu_ref, wd_ref, g_ref, b_ref, o_ref, act_ref, *, alpha, chunk):
    h = h_ref[...]
    hb = h.astype(BF16)
    for c in range(act_ref.shape[1] // chunk):
        cols = slice(c * chunk, (c + 1) * chunk)
        gate = _dot(hb, wg_ref[:, cols])
        up = _dot(hb, wu_ref[:, cols])
        act_ref[:, cols] = (gate / (1.0 + jnp.exp(-gate)) * up).astype(BF16)
    ffn = _dot(act_ref[...], wd_ref[...])
    o_ref[...] = _layer_norm(alpha * h + ffn, g_ref[...], b_ref[...])


def _ffn_ln(h, w_gate, w_up, w_down, ln_g, ln_b, alpha, tm, chunk):
    n, d = h.shape
    d_ff = w_gate.shape[1]
    row = pl.BlockSpec((tm, d), lambda i: (i, 0))
    const = lambda a: pl.BlockSpec(a.shape, lambda i: (0, 0), pipeline_mode=pl.Buffered(1))
    return pl.pallas_call(
        functools.partial(_ffn_ln_kernel, alpha=alpha, chunk=chunk),
        grid=(n // tm,),
        in_specs=[row, const(w_gate), const(w_up), const(w_down), const(ln_g), const(ln_b)],
        out_specs=row,
        out_shape=jax.ShapeDtypeStruct((n, d), F32),
        scratch_shapes=[pltpu.VMEM((tm, d_ff), BF16)],
        compiler_params=_params(("parallel",)),
        name="ffn_ln",
    )(h, w_gate, w_up, w_down, ln_g, ln_b)


def kernel(x, w_in, g_sb, g_dil, w_out, ln1_g, ln1_b, w_gate, w_up, w_down, ln2_g, ln2_b):
    b, s, d = x.shape
    depth = w_in.shape[0]
    sb_width = g_sb.shape[1]
    dil_width = g_dil.shape[1]
    n_dil_heads = dil_width // HEAD_DIM
    alpha = (2.0 * depth) ** 0.25
    slopes = jnp.exp2(-8.0 * (jnp.arange(n_dil_heads, dtype=F32) + 1.0) / n_dil_heads)
    assert s % SUPER == 0

    h = x.reshape(b * s, d)
    for l in range(depth):
        w = w_in[l].astype(BF16)
        w_tok = jnp.concatenate([w[:, :2 * sb_width], w[:, 3 * sb_width:]], axis=1)
        w_vt = w[:, 2 * sb_width:3 * sb_width].T
        qk_a, vt_a, qkv_b = _in_proj(h, w_tok, w_vt, wa=2 * sb_width, tm=512)
        mixed_a = _sb_attention(qk_a.reshape(b, s, -1), vt_a, g_sb[l][None, :], tq=256,
                                width=4 * LANES)
        mixed_b = _dil_attention(qkv_b.reshape(b, s, -1), slopes, g_dil[l][None, :])
        h = _mix_ln(h, mixed_a.reshape(b * s, sb_width), mixed_b.reshape(b * s, dil_width),
                    w_out[l].astype(BF16), ln1_g[l][None, :], ln1_b[l][None, :], alpha, tm=512)
        h = _ffn_ln(h, w_gate[l].astype(BF16), w_up[l].astype(BF16), w_down[l].astype(BF16),
                    ln2_g[l][None, :], ln2_b[l][None, :], alpha, tm=512, chunk=256)
    return h.reshape(b, s, d)
```

```python
import functools

import jax
import jax.numpy as jnp
from jax import lax
from jax.experimental import pallas as pl
from jax.experimental.pallas import tpu as pltpu

HEAD_DIM = 64
LANES = 128
PAIR = LANES // HEAD_DIM
DILATIONS = (1, 4, 16)
WINDOW_KEYS = 128
SUB = WINDOW_KEYS
SUPER = SUB * max(DILATIONS)
GROUP = 2
LN_EPS = 1e-5
RMS_EPS = 1e-6
NEG_BIG = -1e30
F32_EXP_UNDERFLOW = -104.0
LOG2_E = 1.4426950408889634
VMEM_LIMIT = 48 * 1024 * 1024

F32 = jnp.float32
BF16 = jnp.bfloat16


def _dot(a, b):
    return jnp.dot(a, b, preferred_element_type=F32)


def _dot_nt(a, b):
    return lax.dot_general(a, b, (((1,), (1,)), ((), ())), preferred_element_type=F32)


def _split_dot(x, m):
    hi = x.astype(BF16)
    lo = (x - hi.astype(F32)).astype(BF16)
    return _dot(hi, m) + _dot(lo, m)


def _layer_norm(y, g, b):
    mu = jnp.mean(y, axis=-1, keepdims=True)
    d = y - mu
    var = jnp.mean(d * d, axis=-1, keepdims=True)
    return d * lax.rsqrt(var + LN_EPS) * g + b


def _head_rms(o, group_ones, gain):
    ms = _split_dot(o * o, group_ones) * (1.0 / HEAD_DIM)
    return o * lax.rsqrt(ms + RMS_EPS) * gain


def _stack_heads(q):
    lane = lax.broadcasted_iota(jnp.int32, q.shape, 1)
    q = q * jnp.asarray(HEAD_DIM ** -0.5, BF16)
    zero = jnp.zeros_like(q)
    return jnp.concatenate([jnp.where(lane < HEAD_DIM, q, zero),
                            jnp.where(lane < HEAD_DIM, zero, q)], axis=0)


def _unstack_heads(x):
    rows = x.shape[0] // PAIR
    lane = lax.broadcasted_iota(jnp.int32, (rows, LANES), 1)
    return jnp.where(lane < HEAD_DIM, jnp.broadcast_to(x[:rows], (rows, LANES)),
                     jnp.broadcast_to(x[rows:], (rows, LANES)))


def _exp(sign, x):
    return jnp.exp2(x * (sign * LOG2_E))


def _group_ones(width):
    g = jnp.arange(width) // HEAD_DIM
    return (g[:, None] == g[None, :]).astype(BF16)


def _params(sem):
    return pltpu.CompilerParams(dimension_semantics=sem, vmem_limit_bytes=VMEM_LIMIT)


def _in_proj_kernel(x_ref, w_ref, wvt_ref, a_ref, vt_ref, b_ref):
    xb = x_ref[...].astype(BF16)
    proj = _dot(xb, w_ref[...])
    wa = a_ref.shape[1]
    a_ref[...] = proj[:, :wa].astype(a_ref.dtype)
    b_ref[...] = proj[:, wa:]
    vt_ref[...] = _dot_nt(wvt_ref[...], xb).astype(vt_ref.dtype)


def _in_proj(x2d, w, wvt, wa, tm):
    n, d = x2d.shape
    wb = w.shape[1] - wa
    wv = wvt.shape[0]
    return pl.pallas_call(
        _in_proj_kernel,
        grid=(n // tm,),
        in_specs=[pl.BlockSpec((tm, d), lambda i: (i, 0)),
                  pl.BlockSpec(w.shape, lambda i: (0, 0)),
                  pl.BlockSpec(wvt.shape, lambda i: (0, 0))],
        out_specs=[pl.BlockSpec((tm, wa), lambda i: (i, 0)),
                   pl.BlockSpec((wv, tm), lambda i: (0, i)),
                   pl.BlockSpec((tm, wb), lambda i: (i, 0))],
        out_shape=[jax.ShapeDtypeStruct((n, wa), BF16), jax.ShapeDtypeStruct((wv, n), BF16),
                   jax.ShapeDtypeStruct((n, wb), F32)],
        compiler_params=_params(("parallel",)),
        name="in_proj",
    )(x2d, w, wvt)


def _sb_kernel(q_ref, k_ref, vt_ref, ntri_ref, ones_ref, g_ref, o_ref, acc_ref, carry_ref, *, tq):
    i = pl.program_id(2)
    pairs = range(q_ref.shape[2] // LANES)
    cols = [slice(t * LANES, (t + 1) * LANES) for t in pairs]
    q2 = [_stack_heads(q_ref[0, :, cols[t]]) for t in pairs]

    acc_ref[...] = jnp.zeros_like(acc_ref)
    carry_ref[...] = jnp.zeros_like(carry_ref)

    def tile(kb, diagonal):
        start = pl.multiple_of(kb * tq, tq)
        if diagonal:
            key = lax.broadcasted_iota(jnp.int32, (tq, 2 * tq), 0)
            col = lax.broadcasted_iota(jnp.int32, (tq, 2 * tq), 1)
            keep = key < jnp.where(col >= tq, col - tq, col)
        k_blks = [k_ref[0, pl.ds(start, tq), cols[t]] for t in pairs]
        vt_blks = [vt_ref[cols[t], pl.ds(start, tq)] for t in pairs]
        ntri = ntri_ref[...]
        acc_old = [acc_ref[t] for t in pairs]
        carry_old = [carry_ref[t] for t in pairs]
        zs = [_dot_nt(k_blks[t], q2[t]) for t in pairs]
        log_betas, afters, carry_new = [], [], []
        for t in pairs:
            z = zs[t]
            stay = jnp.maximum(z, 0.0) + jnp.log(1.0 + _exp(-1.0, jnp.abs(z)))
            log_betas.append(z - stay)
            if diagonal:
                stay = jnp.where(keep, stay, 0.0)
            hi = stay.astype(BF16)
            lo = (stay - hi.astype(F32)).astype(BF16)
            afters.append(_dot(ntri, jnp.concatenate([hi, lo], axis=0)))
            carry_new.append(carry_old[t] - jnp.sum(stay, axis=0, keepdims=True))
        for t in pairs:
            w = _exp(1.0, log_betas[t] + afters[t] + carry_old[t])
            if diagonal:
                w = jnp.where(keep, w, 0.0)
            acc_ref[t] = acc_old[t] + _dot(vt_blks[t], w.astype(BF16))
            carry_ref[t] = carry_new[t]

    tile(i, True)

    def more(state):
        kb, top = state
        return jnp.logical_and(kb >= 0, top > F32_EXP_UNDERFLOW)

    def body(state):
        kb, _ = state
        tile(kb, False)
        return kb - 1, jnp.max(carry_ref[...])

    lax.while_loop(more, body, (i - 1, jnp.zeros((), F32)))

    feature = lax.broadcasted_iota(jnp.int32, (LANES, tq), 0)
    for t in pairs:
        acc = acc_ref[t]
        o = jnp.where(feature < HEAD_DIM, acc[:, :tq], acc[:, tq:]).T
        o_ref[0, :, cols[t]] = _head_rms(o, ones_ref[...], g_ref[:, cols[t]]).astype(o_ref.dtype)


def _sb_attention(qk, vt, gain, tq, width):
    b, s, _ = qk.shape
    groups = gain.shape[1] // width
    ntri = -(jnp.arange(tq)[None, :] > jnp.arange(tq)[:, None]).astype(BF16)
    ntri = jnp.concatenate([ntri, ntri], axis=1)
    ones = _group_ones(LANES)
    return pl.pallas_call(
        functools.partial(_sb_kernel, tq=tq),
        grid=(b, groups, s // tq),
        in_specs=[pl.BlockSpec((1, tq, width), lambda bi, p, i: (bi, i, p)),
                  pl.BlockSpec((1, s, width), lambda bi, p, i: (bi, 0, groups + p)),
                  pl.BlockSpec((width, s), lambda bi, p, i: (p, bi)),
                  pl.BlockSpec((tq, 2 * tq), lambda bi, p, i: (0, 0)),
                  pl.BlockSpec((LANES, LANES), lambda bi, p, i: (0, 0)),
                  pl.BlockSpec((1, width), lambda bi, p, i: (0, p))],
        out_specs=pl.BlockSpec((1, tq, width), lambda bi, p, i: (bi, i, p)),
        out_shape=jax.ShapeDtypeStruct((b, s, groups * width), BF16),
        scratch_shapes=[pltpu.VMEM((width // LANES, LANES, 2 * tq), F32),
                        pltpu.VMEM((width // LANES, 1, 2 * tq), F32)],
        compiler_params=_params(("parallel", "parallel", "arbitrary")),
        name="sb_attn",
    )(qk, qk, vt, ntri, ones, gain)


def _dil_kernel(slopes_ref, q_ref, kp_ref, kc_ref, vp_ref, vc_ref, ones_ref, g_ref, o_ref,
                bias_ref, o4_ref, l4_ref, o16_ref, l16_ref):
    p = pl.program_id(1)
    first = pl.program_id(2) == 0

    row = lax.broadcasted_iota(jnp.int32, (2 * SUB, 2 * SUB), 0)
    col = lax.broadcasted_iota(jnp.int32, (2 * SUB, 2 * SUB), 1)
    dist = jnp.where(row >= SUB, row - SUB, row) + SUB - col
    in_window = (dist >= 0) & (dist <= WINDOW_KEYS)
    slope = jnp.where(row >= SUB, slopes_ref[PAIR * p + 1], slopes_ref[PAIR * p])
    before_start = jnp.where(col < SUB, jnp.where(first, NEG_BIG, 0.0), 0.0)
    for n, d in enumerate(DILATIONS):
        bias = jnp.where(in_window, -slope * (dist * d).astype(F32), NEG_BIG)
        bias_ref[2 * n] = bias
        bias_ref[2 * n + 1] = bias + before_start

    def rows(ref, start, d):
        if d == 1:
            return ref[0, pl.ds(start, SUB), :]
        return ref[0, pl.ds(start, SUB, stride=d), :]

    def stage(n, start, at_edge):
        d = DILATIONS[n]
        if at_edge:
            k_lo, v_lo = rows(kp_ref, start + SUPER - SUB * d, d), rows(vp_ref, start + SUPER - SUB * d, d)
        else:
            k_lo, v_lo = rows(kc_ref, start - SUB * d, d), rows(vc_ref, start - SUB * d, d)
        kk = jnp.concatenate([k_lo, rows(kc_ref, start, d)], axis=0).astype(BF16)
        vv = jnp.concatenate([v_lo, rows(vc_ref, start, d)], axis=0).astype(BF16)
        return _stack_heads(rows(q_ref, start, d).astype(BF16)), kk, vv

    def softmax_pv(n, at_edge, qk, vv):
        sc = qk + bias_ref[2 * n + (1 if at_edge else 0)]
        m = jnp.max(sc, axis=-1, keepdims=True)
        e = jnp.exp(sc - m)
        l = jnp.sum(e, axis=-1, keepdims=True)
        o = _dot(e.astype(BF16), vv) / l
        lane = lax.broadcasted_iota(jnp.int32, (SUB, LANES), 1)
        return jnp.where(lane < HEAD_DIM, o[:SUB], o[SUB:]), _unstack_heads(m + jnp.log(l))

    def scatter(n, start, o, lse):
        o_s, l_s = (o4_ref, l4_ref) if n == 1 else (o16_ref, l16_ref)
        o_s[pl.ds(start, SUB, stride=DILATIONS[n]), :] = o
        l_s[pl.ds(start, SUB, stride=DILATIONS[n]), :] = lse

    def merge(n, start, o1, l1):
        span = pl.ds(start, SUB)
        l4, l16 = l4_ref[span, :], l16_ref[span, :]
        top = jnp.maximum(jnp.maximum(l1, l4), l16)
        w1, w4, w16 = jnp.exp(l1 - top), jnp.exp(l4 - top), jnp.exp(l16 - top)
        o = (w1 * o1 + w4 * o4_ref[span, :] + w16 * o16_ref[span, :]) / (w1 + w4 + w16)
        o_ref[0, span, :] = _head_rms(o, ones_ref[...], g_ref[...]).astype(o_ref.dtype)

    work = [(2, r, True, scatter) for r in range(DILATIONS[2])]
    work += [(1, j * SUB * DILATIONS[1] + r, j == 0, scatter)
             for j in range(SUPER // (SUB * DILATIONS[1])) for r in range(DILATIONS[1])]
    work += [(0, j * SUB, j == 0, merge) for j in range(SUPER // SUB)]

    def front(item):
        q2, kk, vv = stage(*item[:3])
        return _dot_nt(q2, kk), vv

    fronts = [front(item) for item in work[:GROUP]]
    for t, (n, start, edge, sink) in enumerate(work):
        if t + GROUP < len(work):
            fronts.append(front(work[t + GROUP]))
        qk, vv = fronts[t]
        sink(n, start, *softmax_pv(n, edge, qk, vv))


def _dil_attention(qkv, slopes, gain):
    b, s, _ = qkv.shape
    n_pairs = gain.shape[1] // LANES
    ones = _group_ones(LANES)
    cur = lambda c: (lambda bi, p, i, sl: (bi, i, c * n_pairs + p))
    prev = lambda c: (lambda bi, p, i, sl: (bi, jnp.maximum(i - 1, 0), c * n_pairs + p))
    blk = (1, SUPER, LANES)
    work = pltpu.VMEM((SUPER, LANES), F32)
    return pl.pallas_call(
        _dil_kernel,
        grid_spec=pltpu.PrefetchScalarGridSpec(
            num_scalar_prefetch=1,
            grid=(b, n_pairs, s // SUPER),
            in_specs=[pl.BlockSpec(blk, cur(0)),
                      pl.BlockSpec(blk, prev(1)), pl.BlockSpec(blk, cur(1)),
                      pl.BlockSpec(blk, prev(2)), pl.BlockSpec(blk, cur(2)),
                      pl.BlockSpec((LANES, LANES), lambda bi, p, i, sl: (0, 0)),
                      pl.BlockSpec((1, LANES), lambda bi, p, i, sl: (0, p))],
            out_specs=pl.BlockSpec(blk, cur(0)),
            scratch_shapes=[pltpu.VMEM((2 * len(DILATIONS), 2 * SUB, 2 * SUB), F32),
                            work, work, work, work]),
        out_shape=jax.ShapeDtypeStruct((b, s, n_pairs * LANES), BF16),
        compiler_params=_params(("parallel", "parallel", "arbitrary")),
        name="dil_attn",
    )(slopes, qkv, qkv, qkv, qkv, qkv, ones, gain)


def _post_kernel(x_ref, ma_ref, mb_ref, wo_ref, g1_ref, b1_ref, wg_ref, wu_ref, wd_ref, g2_ref,
                 b2_ref, o_ref, act_ref, *, alpha, chunk):
    half = ma_ref.shape[1]
    mix = _dot(ma_ref[...], wo_ref[:half, :]) + _dot(mb_ref[...], wo_ref[half:, :])
    h = _layer_norm(alpha * x_ref[...] + mix, g1_ref[...], b1_ref[...])
    hb = h.astype(BF16)
    for c in range(act_ref.shape[1] // chunk):
        cols = slice(c * chunk, (c + 1) * chunk)
        gate = _dot(hb, wg_ref[:, cols])
        up = _dot(hb, wu_ref[:, cols])
        act_ref[:, cols] = (gate / (1.0 + jnp.exp(-gate)) * up).astype(BF16)
    ffn = _dot(act_ref[...], wd_ref[...])
    o_ref[...] = _layer_norm(alpha * h + ffn, g2_ref[...], b2_ref[...])


def _post(x2d, ma, mb, w_out, ln1_g, ln1_b, w_gate, w_up, w_down, ln2_g, ln2_b, alpha, tm, chunk):
    n, d = x2d.shape
    d_ff = w_gate.shape[1]
    row = lambda w: pl.BlockSpec((tm, w), lambda i: (i, 0))
    const = lambda a: pl.BlockSpec(a.shape, lambda i: (0, 0), pipeline_mode=pl.Buffered(1))
    consts = (w_out, ln1_g, ln1_b, w_gate, w_up, w_down, ln2_g, ln2_b)
    return pl.pallas_call(
        functools.partial(_post_kernel, alpha=alpha, chunk=chunk),
        grid=(n // tm,),
        in_specs=[row(d), row(ma.shape[1]), row(mb.shape[1])] + [const(a) for a in consts],
        out_specs=row(d),
        out_shape=jax.ShapeDtypeStruct((n, d), F32),
        scratch_shapes=[pltpu.VMEM((tm, d_ff), BF16)],
        compiler_params=_params(("parallel",)),
        name="post",
    )(x2d, ma, mb, *consts)


def kernel(x, w_in, g_sb, g_dil, w_out, ln1_g, ln1_b, w_gate, w_up, w_down, ln2_g, ln2_b):
    b, s, d = x.shape
    depth = w_in.shape[0]
    sb_width = g_sb.shape[1]
    dil_width = g_dil.shape[1]
    n_dil_heads = dil_width // HEAD_DIM
    alpha = (2.0 * depth) ** 0.25
    slopes = jnp.exp2(-8.0 * (jnp.arange(n_dil_heads, dtype=F32) + 1.0) / n_dil_heads)
    assert s % SUPER == 0

    h = x.reshape(b * s, d)
    for l in range(depth):
        w = w_in[l].astype(BF16)
        w_tok = jnp.concatenate([w[:, :2 * sb_width], w[:, 3 * sb_width:]], axis=1)
        w_vt = w[:, 2 * sb_width:3 * sb_width].T
        qk_a, vt_a, qkv_b = _in_proj(h, w_tok, w_vt, wa=2 * sb_width, tm=512)
        mixed_a = _sb_attention(qk_a.reshape(b, s, -1), vt_a, g_sb[l][None, :], tq=256,
                                width=4 * LANES)
        mixed_b = _dil_attention(qkv_b.reshape(b, s, -1), slopes, g_dil[l][None, :])
        h = _post(h, mixed_a.reshape(b * s, sb_width), mixed_b.reshape(b * s, dil_width),
                  w_out[l].astype(BF16), ln1_g[l][None, :], ln1_b[l][None, :],
                  w_gate[l].astype(BF16), w_up[l].astype(BF16), w_down[l].astype(BF16),
                  ln2_g[l][None, :], ln2_b[l][None, :], alpha, tm=512, chunk=256)
    return h.reshape(b, s, d)
```

```python
import functools

import jax
import jax.numpy as jnp
from jax import lax
from jax.experimental import pallas as pl
from jax.experimental.pallas import tpu as pltpu

HEAD_DIM = 64
LANES = 128
PAIR = LANES // HEAD_DIM
DILATIONS = (1, 4, 16)
WINDOW_KEYS = 128
SUB = WINDOW_KEYS
SUPER = SUB * max(DILATIONS)
GROUP = 2
LN_EPS = 1e-5
RMS_EPS = 1e-6
NEG_BIG = -1e30
F32_EXP_UNDERFLOW = -104.0
LOG2_E = 1.4426950408889634
VMEM_LIMIT = 48 * 1024 * 1024
POST_VMEM_LIMIT = 58 * 1024 * 1024

F32 = jnp.float32
BF16 = jnp.bfloat16


def _dot(a, b):
    return jnp.dot(a, b, preferred_element_type=F32)


def _dot_nt(a, b):
    return lax.dot_general(a, b, (((1,), (1,)), ((), ())), preferred_element_type=F32)


def _split_dot(x, m):
    hi = x.astype(BF16)
    lo = (x - hi.astype(F32)).astype(BF16)
    return _dot(hi, m) + _dot(lo, m)


def _layer_norm(y, g, b):
    mu = jnp.mean(y, axis=-1, keepdims=True)
    d = y - mu
    var = jnp.mean(d * d, axis=-1, keepdims=True)
    return d * lax.rsqrt(var + LN_EPS) * g + b


def _head_rms(o, group_ones, gain):
    ms = _split_dot(o * o, group_ones) * (1.0 / HEAD_DIM)
    return o * lax.rsqrt(ms + RMS_EPS) * gain


def _stack_heads(q):
    lane = lax.broadcasted_iota(jnp.int32, q.shape, 1)
    q = q * jnp.asarray(HEAD_DIM ** -0.5, BF16)
    zero = jnp.zeros_like(q)
    return jnp.concatenate([jnp.where(lane < HEAD_DIM, q, zero),
                            jnp.where(lane < HEAD_DIM, zero, q)], axis=0)


def _unstack_heads(x):
    rows = x.shape[0] // PAIR
    lane = lax.broadcasted_iota(jnp.int32, (rows, LANES), 1)
    return jnp.where(lane < HEAD_DIM, jnp.broadcast_to(x[:rows], (rows, LANES)),
                     jnp.broadcast_to(x[rows:], (rows, LANES)))


def _exp(sign, x):
    return jnp.exp2(x * (sign * LOG2_E))


def _group_ones(width):
    g = jnp.arange(width) // HEAD_DIM
    return (g[:, None] == g[None, :]).astype(BF16)


def _params(sem, vmem_limit=VMEM_LIMIT):
    return pltpu.CompilerParams(dimension_semantics=sem, vmem_limit_bytes=vmem_limit)


def _in_proj_kernel(x_ref, w_ref, wvt_ref, a_ref, vt_ref, b_ref):
    xb = x_ref[...].astype(BF16)
    proj = _dot(xb, w_ref[...])
    wa = a_ref.shape[1]
    a_ref[...] = proj[:, :wa].astype(a_ref.dtype)
    b_ref[...] = proj[:, wa:]
    vt_ref[...] = _dot_nt(wvt_ref[...], xb).astype(vt_ref.dtype)


def _in_proj(x2d, w, wvt, wa, tm):
    n, d = x2d.shape
    wb = w.shape[1] - wa
    wv = wvt.shape[0]
    return pl.pallas_call(
        _in_proj_kernel,
        grid=(n // tm,),
        in_specs=[pl.BlockSpec((tm, d), lambda i: (i, 0)),
                  pl.BlockSpec(w.shape, lambda i: (0, 0)),
                  pl.BlockSpec(wvt.shape, lambda i: (0, 0))],
        out_specs=[pl.BlockSpec((tm, wa), lambda i: (i, 0)),
                   pl.BlockSpec((wv, tm), lambda i: (0, i)),
                   pl.BlockSpec((tm, wb), lambda i: (i, 0))],
        out_shape=[jax.ShapeDtypeStruct((n, wa), BF16), jax.ShapeDtypeStruct((wv, n), BF16),
                   jax.ShapeDtypeStruct((n, wb), F32)],
        compiler_params=_params(("parallel",)),
        name="in_proj",
    )(x2d, w, wvt)


def _sb_kernel(q_ref, k_ref, vt_ref, ntri_ref, ones_ref, g_ref, o_ref, acc_ref, carry_ref, *, tq):
    i = pl.program_id(2)
    pairs = range(q_ref.shape[2] // LANES)
    cols = [slice(t * LANES, (t + 1) * LANES) for t in pairs]
    q2 = [_stack_heads(q_ref[0, :, cols[t]]) for t in pairs]

    acc_ref[...] = jnp.zeros_like(acc_ref)
    carry_ref[...] = jnp.zeros_like(carry_ref)

    def tile(kb, diagonal):
        start = pl.multiple_of(kb * tq, tq)
        if diagonal:
            key = lax.broadcasted_iota(jnp.int32, (tq, 2 * tq), 0)
            col = lax.broadcasted_iota(jnp.int32, (tq, 2 * tq), 1)
            keep = key < jnp.where(col >= tq, col - tq, col)
        k_blks = [k_ref[0, pl.ds(start, tq), cols[t]] for t in pairs]
        vt_blks = [vt_ref[cols[t], pl.ds(start, tq)] for t in pairs]
        ntri = ntri_ref[...]
        acc_old = [acc_ref[t] for t in pairs]
        carry_old = [carry_ref[t] for t in pairs]
        zs = [_dot_nt(k_blks[t], q2[t]) for t in pairs]
        log_betas, afters, carry_new = [], [], []
        for t in pairs:
            z = zs[t]
            stay = jnp.maximum(z, 0.0) + jnp.log(1.0 + _exp(-1.0, jnp.abs(z)))
            log_betas.append(z - stay)
            if diagonal:
                stay = jnp.where(keep, stay, 0.0)
            hi = stay.astype(BF16)
            lo = (stay - hi.astype(F32)).astype(BF16)
            afters.append(_dot(ntri, jnp.concatenate([hi, lo], axis=0)))
            carry_new.append(carry_old[t] - jnp.sum(stay, axis=0, keepdims=True))
        for t in pairs:
            w = _exp(1.0, log_betas[t] + afters[t] + carry_old[t])
            if diagonal:
                w = jnp.where(keep, w, 0.0)
            acc_ref[t] = acc_old[t] + _dot(vt_blks[t], w.astype(BF16))
            carry_ref[t] = carry_new[t]

    tile(i, True)

    def more(state):
        kb, top = state
        return jnp.logical_and(kb >= 0, top > F32_EXP_UNDERFLOW)

    def body(state):
        kb, _ = state
        tile(kb, False)
        return kb - 1, jnp.max(carry_ref[...])

    lax.while_loop(more, body, (i - 1, jnp.zeros((), F32)))

    feature = lax.broadcasted_iota(jnp.int32, (LANES, tq), 0)
    for t in pairs:
        acc = acc_ref[t]
        o = jnp.where(feature < HEAD_DIM, acc[:, :tq], acc[:, tq:]).T
        o_ref[0, :, cols[t]] = _head_rms(o, ones_ref[...], g_ref[:, cols[t]]).astype(o_ref.dtype)


def _sb_attention(qk, vt, gain, tq, width):
    b, s, _ = qk.shape
    groups = gain.shape[1] // width
    ntri = -(jnp.arange(tq)[None, :] > jnp.arange(tq)[:, None]).astype(BF16)
    ntri = jnp.concatenate([ntri, ntri], axis=1)
    ones = _group_ones(LANES)
    return pl.pallas_call(
        functools.partial(_sb_kernel, tq=tq),
        grid=(b, groups, s // tq),
        in_specs=[pl.BlockSpec((1, tq, width), lambda bi, p, i: (bi, i, p)),
                  pl.BlockSpec((1, s, width), lambda bi, p, i: (bi, 0, groups + p)),
                  pl.BlockSpec((width, s), lambda bi, p, i: (p, bi)),
                  pl.BlockSpec((tq, 2 * tq), lambda bi, p, i: (0, 0)),
                  pl.BlockSpec((LANES, LANES), lambda bi, p, i: (0, 0)),
                  pl.BlockSpec((1, width), lambda bi, p, i: (0, p))],
        out_specs=pl.BlockSpec((1, tq, width), lambda bi, p, i: (bi, i, p)),
        out_shape=jax.ShapeDtypeStruct((b, s, groups * width), BF16),
        scratch_shapes=[pltpu.VMEM((width // LANES, LANES, 2 * tq), F32),
                        pltpu.VMEM((width // LANES, 1, 2 * tq), F32)],
        compiler_params=_params(("parallel", "parallel", "arbitrary")),
        name="sb_attn",
    )(qk, qk, vt, ntri, ones, gain)


def _dil_kernel(slopes_ref, q_ref, kp_ref, kc_ref, vp_ref, vc_ref, ones_ref, g_ref, o_ref,
                bias_ref, by4_ref, o4_ref, l4_ref, o16_ref, l16_ref):
    p = pl.program_id(1)
    first = pl.program_id(2) == 0

    row = lax.broadcasted_iota(jnp.int32, (2 * SUB, 2 * SUB), 0)
    col = lax.broadcasted_iota(jnp.int32, (2 * SUB, 2 * SUB), 1)
    dist = jnp.where(row >= SUB, row - SUB, row) + SUB - col
    in_window = (dist >= 0) & (dist <= WINDOW_KEYS)
    slope = jnp.where(row >= SUB, slopes_ref[PAIR * p + 1], slopes_ref[PAIR * p])
    before_start = jnp.where(col < SUB, jnp.where(first, NEG_BIG, 0.0), 0.0)
    for n, d in enumerate(DILATIONS):
        bias = jnp.where(in_window, -slope * (dist * d).astype(F32), NEG_BIG)
        bias_ref[2 * n] = bias
        bias_ref[2 * n + 1] = bias + before_start

    Q, KP, KC, VP, VC = range(5)
    token_order = (q_ref, kp_ref, kc_ref, vp_ref, vc_ref)
    quarter = SUPER // 4
    for a, src in enumerate(token_order):
        for c in range(4):
            by4_ref[a, c * quarter:(c + 1) * quarter, :] = src[0, pl.ds(c, quarter, stride=4), :]

    def rows(a, start, d):
        if d == 1:
            return token_order[a][0, pl.ds(start, SUB), :]
        first_row = (start % 4) * quarter + start // 4
        if d == 4:
            return by4_ref[a, pl.ds(first_row, SUB), :]
        return by4_ref[a, pl.ds(first_row, SUB, stride=d // 4), :]

    def stage(n, start, at_edge):
        d = DILATIONS[n]
        if at_edge:
            k_lo, v_lo = rows(KP, start + SUPER - SUB * d, d), rows(VP, start + SUPER - SUB * d, d)
        else:
            k_lo, v_lo = rows(KC, start - SUB * d, d), rows(VC, start - SUB * d, d)
        kk = jnp.concatenate([k_lo, rows(KC, start, d)], axis=0).astype(BF16)
        vv = jnp.concatenate([v_lo, rows(VC, start, d)], axis=0).astype(BF16)
        return _stack_heads(rows(Q, start, d).astype(BF16)), kk, vv

    def softmax_pv(n, at_edge, qk, vv):
        sc = qk + bias_ref[2 * n + (1 if at_edge else 0)]
        m = jnp.max(sc, axis=-1, keepdims=True)
        e = jnp.exp(sc - m)
        l = jnp.sum(e, axis=-1, keepdims=True)
        o = _dot(e.astype(BF16), vv) / l
        lane = lax.broadcasted_iota(jnp.int32, (SUB, LANES), 1)
        return jnp.where(lane < HEAD_DIM, o[:SUB], o[SUB:]), _unstack_heads(m + jnp.log(l))

    def scatter(n, start, o, lse):
        o_s, l_s = (o4_ref, l4_ref) if n == 1 else (o16_ref, l16_ref)
        o_s[pl.ds(start, SUB, stride=DILATIONS[n]), :] = o
        l_s[pl.ds(start, SUB, stride=DILATIONS[n]), :] = lse

    def merge(n, start, o1, l1):
        span = pl.ds(start, SUB)
        l4, l16 = l4_ref[span, :], l16_ref[span, :]
        top = jnp.maximum(jnp.maximum(l1, l4), l16)
        w1, w4, w16 = jnp.exp(l1 - top), jnp.exp(l4 - top), jnp.exp(l16 - top)
        o = (w1 * o1 + w4 * o4_ref[span, :] + w16 * o16_ref[span, :]) / (w1 + w4 + w16)
        o_ref[0, span, :] = _head_rms(o, ones_ref[...], g_ref[...]).astype(o_ref.dtype)

    work = [(2, r, True, scatter) for r in range(DILATIONS[2])]
    work += [(1, j * SUB * DILATIONS[1] + r, j == 0, scatter)
             for j in range(SUPER // (SUB * DILATIONS[1])) for r in range(DILATIONS[1])]
    work += [(0, j * SUB, j == 0, merge) for j in range(SUPER // SUB)]

    def front(item):
        q2, kk, vv = stage(*item[:3])
        return _dot_nt(q2, kk), vv

    fronts = [front(item) for item in work[:GROUP]]
    for t, (n, start, edge, sink) in enumerate(work):
        if t + GROUP < len(work):
            fronts.append(front(work[t + GROUP]))
        qk, vv = fronts[t]
        sink(n, start, *softmax_pv(n, edge, qk, vv))


def _dil_attention(qkv, slopes, gain):
    b, s, _ = qkv.shape
    n_pairs = gain.shape[1] // LANES
    ones = _group_ones(LANES)
    cur = lambda c: (lambda bi, p, i, sl: (bi, i, c * n_pairs + p))
    prev = lambda c: (lambda bi, p, i, sl: (bi, jnp.maximum(i - 1, 0), c * n_pairs + p))
    blk = (1, SUPER, LANES)
    work = pltpu.VMEM((SUPER, LANES), F32)
    return pl.pallas_call(
        _dil_kernel,
        grid_spec=pltpu.PrefetchScalarGridSpec(
            num_scalar_prefetch=1,
            grid=(b, n_pairs, s // SUPER),
            in_specs=[pl.BlockSpec(blk, cur(0)),
                      pl.BlockSpec(blk, prev(1)), pl.BlockSpec(blk, cur(1)),
                      pl.BlockSpec(blk, prev(2)), pl.BlockSpec(blk, cur(2)),
                      pl.BlockSpec((LANES, LANES), lambda bi, p, i, sl: (0, 0)),
                      pl.BlockSpec((1, LANES), lambda bi, p, i, sl: (0, p))],
            out_specs=pl.BlockSpec(blk, cur(0)),
            scratch_shapes=[pltpu.VMEM((2 * len(DILATIONS), 2 * SUB, 2 * SUB), F32),
                            pltpu.VMEM((5, SUPER, LANES), F32), work, work, work, work]),
        out_shape=jax.ShapeDtypeStruct((b, s, n_pairs * LANES), BF16),
        compiler_params=_params(("parallel", "parallel", "arbitrary")),
        name="dil_attn",
    )(slopes, qkv, qkv, qkv, qkv, qkv, ones, gain)


def _post_kernel(x_ref, ma_ref, mb_ref, wo_ref, g1_ref, b1_ref, wg_ref, wu_ref, wd_ref, g2_ref,
                 b2_ref, o_ref, act_ref, *, alpha, chunk, parts):
    half = ma_ref.shape[1]
    rows = x_ref.shape[0] // parts
    spans = [slice(r * rows, (r + 1) * rows) for r in range(parts)]
    mixes = [_dot(ma_ref[sp, :], wo_ref[:half, :]) + _dot(mb_ref[sp, :], wo_ref[half:, :])
             for sp in spans]
    hs = [_layer_norm(alpha * x_ref[sp, :] + mix, g1_ref[...], b1_ref[...])
          for sp, mix in zip(spans, mixes)]
    for r, (sp, h) in enumerate(zip(spans, hs)):
        hb = h.astype(BF16)
        for c in range(act_ref.shape[2] // chunk):
            cols = slice(c * chunk, (c + 1) * chunk)
            gate = _dot(hb, wg_ref[:, cols])
            up = _dot(hb, wu_ref[:, cols])
            act_ref[r, :, cols] = (gate / (1.0 + jnp.exp(-gate)) * up).astype(BF16)
        ffn = _dot(act_ref[r], wd_ref[...])
        o_ref[sp, :] = _layer_norm(alpha * h + ffn, g2_ref[...], b2_ref[...])


def _post(x2d, ma, mb, w_out, ln1_g, ln1_b, w_gate, w_up, w_down, ln2_g, ln2_b, alpha, tm, chunk,
          parts):
    n, d = x2d.shape
    d_ff = w_gate.shape[1]
    row = lambda w: pl.BlockSpec((tm, w), lambda i: (i, 0))
    const = lambda a: pl.BlockSpec(a.shape, lambda i: (0, 0), pipeline_mode=pl.Buffered(1))
    consts = (w_out, ln1_g, ln1_b, w_gate, w_up, w_down, ln2_g, ln2_b)
    return pl.pallas_call(
        functools.partial(_post_kernel, alpha=alpha, chunk=chunk, parts=parts),
        grid=(n // tm,),
        in_specs=[row(d), row(ma.shape[1]), row(mb.shape[1])] + [const(a) for a in consts],
        out_specs=row(d),
        out_shape=jax.ShapeDtypeStruct((n, d), F32),
        scratch_shapes=[pltpu.VMEM((parts, tm // parts, d_ff), BF16)],
        compiler_params=_params(("parallel",), POST_VMEM_LIMIT),
        name="post",
    )(x2d, ma, mb, *consts)


def kernel(x, w_in, g_sb, g_dil, w_out, ln1_g, ln1_b, w_gate, w_up, w_down, ln2_g, ln2_b):
    b, s, d = x.shape
    depth = w_in.shape[0]
    sb_width = g_sb.shape[1]
    dil_width = g_dil.shape[1]
    n_dil_heads = dil_width // HEAD_DIM
    alpha = (2.0 * depth) ** 0.25
    slopes = jnp.exp2(-8.0 * (jnp.arange(n_dil_heads, dtype=F32) + 1.0) / n_dil_heads)
    assert s % SUPER == 0

    h = x.reshape(b * s, d)
    for l in range(depth):
        w = w_in[l].astype(BF16)
        w_tok = jnp.concatenate([w[:, :2 * sb_width], w[:, 3 * sb_width:]], axis=1)
        w_vt = w[:, 2 * sb_width:3 * sb_width].T
        qk_a, vt_a, qkv_b = _in_proj(h, w_tok, w_vt, wa=2 * sb_width, tm=512)
        mixed_a = _sb_attention(qk_a.reshape(b, s, -1), vt_a, g_sb[l][None, :], tq=256,
                                width=4 * LANES)
        mixed_b = _dil_attention(qkv_b.reshape(b, s, -1), slopes, g_dil[l][None, :])
        h = _post(h, mixed_a.reshape(b * s, sb_width), mixed_b.reshape(b * s, dil_width),
                  w_out[l].astype(BF16), ln1_g[l][None, :], ln1_b[l][None, :],
                  w_gate[l].astype(BF16), w_up[l].astype(BF16), w_down[l].astype(BF16),
                  ln2_g[l][None, :], ln2_b[l][None, :], alpha, tm=1024, chunk=256, parts=2)
    return h.reshape(b, s, d)
```

```python
import functools

import jax
import jax.numpy as jnp
from jax import lax
from jax.experimental import pallas as pl
from jax.experimental.pallas import tpu as pltpu

HEAD_DIM = 64
LANES = 128
PAIR = LANES // HEAD_DIM
DILATIONS = (1, 4, 16)
WINDOW_KEYS = 128
SUB = WINDOW_KEYS
SUPER = SUB * max(DILATIONS)
GROUP = 2
LN_EPS = 1e-5
RMS_EPS = 1e-6
NEG_BIG = -1e30
F32_EXP_UNDERFLOW = -104.0
LOG2_E = 1.4426950408889634
VMEM_LIMIT = 48 * 1024 * 1024
POST_VMEM_LIMIT = 58 * 1024 * 1024

F32 = jnp.float32
BF16 = jnp.bfloat16


def _dot(a, b):
    return jnp.dot(a, b, preferred_element_type=F32)


def _dot_nt(a, b):
    return lax.dot_general(a, b, (((1,), (1,)), ((), ())), preferred_element_type=F32)


def _split_dot(x, m):
    hi = x.astype(BF16)
    lo = (x - hi.astype(F32)).astype(BF16)
    return _dot(hi, m) + _dot(lo, m)


def _layer_norm(y, g, b):
    mu = jnp.mean(y, axis=-1, keepdims=True)
    d = y - mu
    var = jnp.mean(d * d, axis=-1, keepdims=True)
    return d * lax.rsqrt(var + LN_EPS) * g + b


def _head_rms(o, group_ones, gain):
    ms = _split_dot(o * o, group_ones) * (1.0 / HEAD_DIM)
    return o * lax.rsqrt(ms + RMS_EPS) * gain


def _stack_heads(q):
    lane = lax.broadcasted_iota(jnp.int32, q.shape, 1)
    q = q * jnp.asarray(HEAD_DIM ** -0.5, BF16)
    zero = jnp.zeros_like(q)
    return jnp.concatenate([jnp.where(lane < HEAD_DIM, q, zero),
                            jnp.where(lane < HEAD_DIM, zero, q)], axis=0)


def _unstack_heads(x):
    rows = x.shape[0] // PAIR
    lane = lax.broadcasted_iota(jnp.int32, (rows, LANES), 1)
    return jnp.where(lane < HEAD_DIM, jnp.broadcast_to(x[:rows], (rows, LANES)),
                     jnp.broadcast_to(x[rows:], (rows, LANES)))


def _exp(sign, x):
    return jnp.exp2(x * (sign * LOG2_E))


def _group_ones(width):
    g = jnp.arange(width) // HEAD_DIM
    return (g[:, None] == g[None, :]).astype(BF16)


def _params(sem, vmem_limit=VMEM_LIMIT):
    return pltpu.CompilerParams(dimension_semantics=sem, vmem_limit_bytes=vmem_limit)


def _in_proj_kernel(x_ref, w_ref, wvt_ref, a_ref, vt_ref, b_ref):
    xb = x_ref[...].astype(BF16)
    proj = _dot(xb, w_ref[...])
    wa = a_ref.shape[1]
    a_ref[...] = proj[:, :wa].astype(a_ref.dtype)
    b_ref[...] = proj[:, wa:]
    vt_ref[...] = _dot_nt(wvt_ref[...], xb).astype(vt_ref.dtype)


def _in_proj(x2d, w, wvt, wa, tm):
    n, d = x2d.shape
    wb = w.shape[1] - wa
    wv = wvt.shape[0]
    return pl.pallas_call(
        _in_proj_kernel,
        grid=(n // tm,),
        in_specs=[pl.BlockSpec((tm, d), lambda i: (i, 0)),
                  pl.BlockSpec(w.shape, lambda i: (0, 0)),
                  pl.BlockSpec(wvt.shape, lambda i: (0, 0))],
        out_specs=[pl.BlockSpec((tm, wa), lambda i: (i, 0)),
                   pl.BlockSpec((wv, tm), lambda i: (0, i)),
                   pl.BlockSpec((tm, wb), lambda i: (i, 0))],
        out_shape=[jax.ShapeDtypeStruct((n, wa), BF16), jax.ShapeDtypeStruct((wv, n), BF16),
                   jax.ShapeDtypeStruct((n, wb), F32)],
        compiler_params=_params(("parallel",)),
        name="in_proj",
    )(x2d, w, wvt)


def _sb_kernel(q_ref, k_ref, vt_ref, ntri_ref, ones_ref, g_ref, o_ref, acc_ref, carry_ref, *, tq):
    i = pl.program_id(2)
    pairs = range(q_ref.shape[2] // LANES)
    cols = [slice(t * LANES, (t + 1) * LANES) for t in pairs]
    q2 = [_stack_heads(q_ref[0, :, cols[t]]) for t in pairs]

    acc_ref[...] = jnp.zeros_like(acc_ref)
    carry_ref[...] = jnp.zeros_like(carry_ref)

    def tile(kb, diagonal):
        start = pl.multiple_of(kb * tq, tq)
        if diagonal:
            key = lax.broadcasted_iota(jnp.int32, (tq, 2 * tq), 0)
            col = lax.broadcasted_iota(jnp.int32, (tq, 2 * tq), 1)
            keep = key < jnp.where(col >= tq, col - tq, col)
        k_blks = [k_ref[0, pl.ds(start, tq), cols[t]] for t in pairs]
        vt_blks = [vt_ref[cols[t], pl.ds(start, tq)] for t in pairs]
        ntri = ntri_ref[...]
        acc_old = [acc_ref[t] for t in pairs]
        carry_old = [carry_ref[t] for t in pairs]
        zs = [_dot_nt(k_blks[t], q2[t]) for t in pairs]
        log_betas, afters, carry_new = [], [], []
        for t in pairs:
            z = zs[t]
            stay = jnp.maximum(z, 0.0) + jnp.log(1.0 + _exp(-1.0, jnp.abs(z)))
            log_betas.append(z - stay)
            if diagonal:
                stay = jnp.where(keep, stay, 0.0)
            hi = stay.astype(BF16)
            lo = (stay - hi.astype(F32)).astype(BF16)
            afters.append(_dot(ntri, jnp.concatenate([hi, lo], axis=0)))
            carry_new.append(carry_old[t] - jnp.sum(stay, axis=0, keepdims=True))
        for t in pairs:
            w = _exp(1.0, log_betas[t] + afters[t] + carry_old[t])
            if diagonal:
                w = jnp.where(keep, w, 0.0)
            acc_ref[t] = acc_old[t] + _dot(vt_blks[t], w.astype(BF16))
            carry_ref[t] = carry_new[t]

    tile(i, True)

    def more(state):
        kb, top = state
        return jnp.logical_and(kb >= 0, top > F32_EXP_UNDERFLOW)

    def body(state):
        kb, _ = state
        tile(kb, False)
        return kb - 1, jnp.max(carry_ref[...])

    lax.while_loop(more, body, (i - 1, jnp.zeros((), F32)))

    feature = lax.broadcasted_iota(jnp.int32, (LANES, tq), 0)
    for t in pairs:
        acc = acc_ref[t]
        o = jnp.where(feature < HEAD_DIM, acc[:, :tq], acc[:, tq:]).T
        o_ref[0, :, cols[t]] = _head_rms(o, ones_ref[...], g_ref[:, cols[t]]).astype(o_ref.dtype)


def _sb_attention(qk, vt, gain, tq, width):
    b, s, _ = qk.shape
    groups = gain.shape[1] // width
    ntri = -(jnp.arange(tq)[None, :] > jnp.arange(tq)[:, None]).astype(BF16)
    ntri = jnp.concatenate([ntri, ntri], axis=1)
    ones = _group_ones(LANES)
    return pl.pallas_call(
        functools.partial(_sb_kernel, tq=tq),
        grid=(b, groups, s // tq),
        in_specs=[pl.BlockSpec((1, tq, width), lambda bi, p, i: (bi, i, p)),
                  pl.BlockSpec((1, s, width), lambda bi, p, i: (bi, 0, groups + p)),
                  pl.BlockSpec((width, s), lambda bi, p, i: (p, bi)),
                  pl.BlockSpec((tq, 2 * tq), lambda bi, p, i: (0, 0)),
                  pl.BlockSpec((LANES, LANES), lambda bi, p, i: (0, 0)),
                  pl.BlockSpec((1, width), lambda bi, p, i: (0, p))],
        out_specs=pl.BlockSpec((1, tq, width), lambda bi, p, i: (bi, i, p)),
        out_shape=jax.ShapeDtypeStruct((b, s, groups * width), BF16),
        scratch_shapes=[pltpu.VMEM((width // LANES, LANES, 2 * tq), F32),
                        pltpu.VMEM((width // LANES, 1, 2 * tq), F32)],
        compiler_params=_params(("parallel", "parallel", "arbitrary")),
        name="sb_attn",
    )(qk, qk, vt, ntri, ones, gain)


def _dil_kernel(slopes_ref, q_ref, kp_ref, kc_ref, vp_ref, vc_ref, ones_ref, g_ref, o_ref,
                bias_ref, by4_ref, o4_ref, l4_ref, o16_ref, l16_ref):
    p = pl.program_id(1)
    first = pl.program_id(2) == 0

    row = lax.broadcasted_iota(jnp.int32, (2 * SUB, 2 * SUB), 0)
    col = lax.broadcasted_iota(jnp.int32, (2 * SUB, 2 * SUB), 1)
    dist = jnp.where(row >= SUB, row - SUB, row) + SUB - col
    in_window = (dist >= 0) & (dist <= WINDOW_KEYS)
    slope = jnp.where(row >= SUB, slopes_ref[PAIR * p + 1], slopes_ref[PAIR * p])
    before_start = jnp.where(col < SUB, jnp.where(first, NEG_BIG, 0.0), 0.0)
    for n, d in enumerate(DILATIONS):
        bias = jnp.where(in_window, -slope * (dist * d).astype(F32), NEG_BIG)
        bias_ref[2 * n] = bias
        bias_ref[2 * n + 1] = bias + before_start

    Q, KP, KC, VP, VC = range(5)
    token_order = (q_ref, kp_ref, kc_ref, vp_ref, vc_ref)
    quarter = SUPER // 4
    for a, src in enumerate(token_order):
        for c in range(4):
            by4_ref[a, c * quarter:(c + 1) * quarter, :] = src[0, pl.ds(c, quarter, stride=4), :]

    def rows(a, start, d):
        if d == 1:
            return token_order[a][0, pl.ds(start, SUB), :]
        first_row = (start % 4) * quarter + start // 4
        if d == 4:
            return by4_ref[a, pl.ds(first_row, SUB), :]
        return by4_ref[a, pl.ds(first_row, SUB, stride=d // 4), :]

    def stage(n, start, at_edge):
        d = DILATIONS[n]
        if at_edge:
            k_lo, v_lo = rows(KP, start + SUPER - SUB * d, d), rows(VP, start + SUPER - SUB * d, d)
        else:
            k_lo, v_lo = rows(KC, start - SUB * d, d), rows(VC, start - SUB * d, d)
        kk = jnp.concatenate([k_lo, rows(KC, start, d)], axis=0).astype(BF16)
        vv = jnp.concatenate([v_lo, rows(VC, start, d)], axis=0).astype(BF16)
        return _stack_heads(rows(Q, start, d).astype(BF16)), kk, vv

    def softmax_pv(n, at_edge, qk, vv):
        sc = qk + bias_ref[2 * n + (1 if at_edge else 0)]
        m = jnp.max(sc, axis=-1, keepdims=True)
        e = jnp.exp(sc - m)
        l = jnp.sum(e, axis=-1, keepdims=True)
        o = _dot(e.astype(BF16), vv) / l
        lane = lax.broadcasted_iota(jnp.int32, (SUB, LANES), 1)
        return jnp.where(lane < HEAD_DIM, o[:SUB], o[SUB:]), _unstack_heads(m + jnp.log(l))

    def scatter(n, start, o, lse):
        o_s, l_s = (o4_ref, l4_ref) if n == 1 else (o16_ref, l16_ref)
        o_s[pl.ds(start, SUB, stride=DILATIONS[n]), :] = o
        l_s[pl.ds(start, SUB, stride=DILATIONS[n]), :] = lse

    def merge(n, start, o1, l1):
        span = pl.ds(start, SUB)
        l4, l16 = l4_ref[span, :], l16_ref[span, :]
        top = jnp.maximum(jnp.maximum(l1, l4), l16)
        w1, w4, w16 = jnp.exp(l1 - top), jnp.exp(l4 - top), jnp.exp(l16 - top)
        o = (w1 * o1 + w4 * o4_ref[span, :] + w16 * o16_ref[span, :]) / (w1 + w4 + w16)
        o_ref[0, span, :] = _head_rms(o, ones_ref[...], g_ref[...]).astype(o_ref.dtype)

    work = [(2, r, True, scatter) for r in range(DILATIONS[2])]
    work += [(1, j * SUB * DILATIONS[1] + r, j == 0, scatter)
             for j in range(SUPER // (SUB * DILATIONS[1])) for r in range(DILATIONS[1])]
    work += [(0, j * SUB, j == 0, merge) for j in range(SUPER // SUB)]

    def front(item):
        q2, kk, vv = stage(*item[:3])
        return _dot_nt(q2, kk), vv

    fronts = [front(item) for item in work[:GROUP]]
    for t, (n, start, edge, sink) in enumerate(work):
        if t + GROUP < len(work):
            fronts.append(front(work[t + GROUP]))
        qk, vv = fronts[t]
        sink(n, start, *softmax_pv(n, edge, qk, vv))


def _dil_attention(qkv, slopes, gain):
    b, s, _ = qkv.shape
    n_pairs = gain.shape[1] // LANES
    ones = _group_ones(LANES)
    cur = lambda c: (lambda bi, p, i, sl: (bi, i, c * n_pairs + p))
    prev = lambda c: (lambda bi, p, i, sl: (bi, jnp.maximum(i - 1, 0), c * n_pairs + p))
    blk = (1, SUPER, LANES)
    work = pltpu.VMEM((SUPER, LANES), F32)
    return pl.pallas_call(
        _dil_kernel,
        grid_spec=pltpu.PrefetchScalarGridSpec(
            num_scalar_prefetch=1,
            grid=(b, n_pairs, s // SUPER),
            in_specs=[pl.BlockSpec(blk, cur(0)),
                      pl.BlockSpec(blk, prev(1)), pl.BlockSpec(blk, cur(1)),
                      pl.BlockSpec(blk, prev(2)), pl.BlockSpec(blk, cur(2)),
                      pl.BlockSpec((LANES, LANES), lambda bi, p, i, sl: (0, 0)),
                      pl.BlockSpec((1, LANES), lambda bi, p, i, sl: (0, p))],
            out_specs=pl.BlockSpec(blk, cur(0)),
            scratch_shapes=[pltpu.VMEM((2 * len(DILATIONS), 2 * SUB, 2 * SUB), F32),
                            pltpu.VMEM((5, SUPER, LANES), F32), work, work, work, work]),
        out_shape=jax.ShapeDtypeStruct((b, s, n_pairs * LANES), BF16),
        compiler_params=_params(("parallel", "parallel", "arbitrary")),
        name="dil_attn",
    )(slopes, qkv, qkv, qkv, qkv, qkv, ones, gain)


def _post_kernel(x_ref, ma_ref, mb_ref, wo_ref, g1_ref, b1_ref, wg_ref, wu_ref, wd_ref, g2_ref,
                 b2_ref, o_ref, act_ref, *, alpha, chunk, parts):
    half = ma_ref.shape[1]
    rows = x_ref.shape[0] // parts
    spans = [slice(r * rows, (r + 1) * rows) for r in range(parts)]
    mixes = [_dot(ma_ref[sp, :], wo_ref[:half, :]) + _dot(mb_ref[sp, :], wo_ref[half:, :])
             for sp in spans]
    hs = [_layer_norm(alpha * x_ref[sp, :] + mix, g1_ref[...], b1_ref[...])
          for sp, mix in zip(spans, mixes)]
    for r, (sp, h) in enumerate(zip(spans, hs)):
        hb = h.astype(BF16)
        for c in range(act_ref.shape[2] // chunk):
            cols = slice(c * chunk, (c + 1) * chunk)
            gate = _dot(hb, wg_ref[:, cols])
            up = _dot(hb, wu_ref[:, cols])
            act_ref[r, :, cols] = (gate / (1.0 + jnp.exp(-gate)) * up).astype(BF16)
        ffn = _dot(act_ref[r], wd_ref[...])
        o_ref[sp, :] = _layer_norm(alpha * h + ffn, g2_ref[...], b2_ref[...])


def _post(x2d, ma, mb, w_out, ln1_g, ln1_b, w_gate, w_up, w_down, ln2_g, ln2_b, alpha, tm, chunk,
          parts):
    n, d = x2d.shape
    d_ff = w_gate.shape[1]
    row = lambda w: pl.BlockSpec((tm, w), lambda i: (i, 0))
    const = lambda a: pl.BlockSpec(a.shape, lambda i: (0, 0), pipeline_mode=pl.Buffered(1))
    consts = (w_out, ln1_g, ln1_b, w_gate, w_up, w_down, ln2_g, ln2_b)
    return pl.pallas_call(
        functools.partial(_post_kernel, alpha=alpha, chunk=chunk, parts=parts),
        grid=(n // tm,),
        in_specs=[row(d), row(ma.shape[1]), row(mb.shape[1])] + [const(a) for a in consts],
        out_specs=row(d),
        out_shape=jax.ShapeDtypeStruct((n, d), F32),
        scratch_shapes=[pltpu.VMEM((parts, tm // parts, d_ff), BF16)],
        compiler_params=_params(("parallel",), POST_VMEM_LIMIT),
        name="post",
    )(x2d, ma, mb, *consts)


def kernel(x, w_in, g_sb, g_dil, w_out, ln1_g, ln1_b, w_gate, w_up, w_down, ln2_g, ln2_b):
    b, s, d = x.shape
    depth = w_in.shape[0]
    sb_width = g_sb.shape[1]
    dil_width = g_dil.shape[1]
    n_dil_heads = dil_width // HEAD_DIM
    alpha = (2.0 * depth) ** 0.25
    slopes = jnp.exp2(-8.0 * (jnp.arange(n_dil_heads, dtype=F32) + 1.0) / n_dil_heads)
    assert s % SUPER == 0

    h = x.reshape(b * s, d)
    for l in range(depth):
        w = w_in[l].astype(BF16)
        w_tok = jnp.concatenate([w[:, :2 * sb_width], w[:, 3 * sb_width:]], axis=1)
        w_vt = w[:, 2 * sb_width:3 * sb_width].T
        qk_a, vt_a, qkv_b = _in_proj(h, w_tok, w_vt, wa=2 * sb_width, tm=1024)
        mixed_a = _sb_attention(qk_a.reshape(b, s, -1), vt_a, g_sb[l][None, :], tq=256,
                                width=4 * LANES)
        mixed_b = _dil_attention(qkv_b.reshape(b, s, -1), slopes, g_dil[l][None, :])
        h = _post(h, mixed_a.reshape(b * s, sb_width), mixed_b.reshape(b * s, dil_width),
                  w_out[l].astype(BF16), ln1_g[l][None, :], ln1_b[l][None, :],
                  w_gate[l].astype(BF16), w_up[l].astype(BF16), w_down[l].astype(BF16),
                  ln2_g[l][None, :], ln2_b[l][None, :], alpha, tm=1024, chunk=256, parts=2)
    return h.reshape(b, s, d)
```

```python
import functools

import jax
import jax.numpy as jnp
from jax import lax
from jax.experimental import pallas as pl
from jax.experimental.pallas import tpu as pltpu

HEAD_DIM = 64
LANES = 128
PAIR = LANES // HEAD_DIM
DILATIONS = (1, 4, 16)
WINDOW_KEYS = 128
SUB = WINDOW_KEYS
SUPER = SUB * max(DILATIONS)
GROUP = 2
LN_EPS = 1e-5
RMS_EPS = 1e-6
NEG_BIG = -1e30
F32_EXP_UNDERFLOW = -104.0
LOG2_E = 1.4426950408889634
VMEM_LIMIT = 48 * 1024 * 1024
POST_VMEM_LIMIT = 58 * 1024 * 1024

F32 = jnp.float32
BF16 = jnp.bfloat16


def _dot(a, b):
    return jnp.dot(a, b, preferred_element_type=F32)


def _dot_nt(a, b):
    return lax.dot_general(a, b, (((1,), (1,)), ((), ())), preferred_element_type=F32)


def _layer_norm(y, g, b):
    mu = jnp.mean(y, axis=-1, keepdims=True)
    d = y - mu
    var = jnp.mean(d * d, axis=-1, keepdims=True)
    return d * lax.rsqrt(var + LN_EPS) * g + b


def _stack_heads(q):
    lane = lax.broadcasted_iota(jnp.int32, q.shape, 1)
    q = q * jnp.asarray(HEAD_DIM ** -0.5, BF16)
    zero = jnp.zeros_like(q)
    return jnp.concatenate([jnp.where(lane < HEAD_DIM, q, zero),
                            jnp.where(lane < HEAD_DIM, zero, q)], axis=0)


def _unstack_heads(x):
    rows = x.shape[0] // PAIR
    lane = lax.broadcasted_iota(jnp.int32, (rows, LANES), 1)
    return jnp.where(lane < HEAD_DIM, jnp.broadcast_to(x[:rows], (rows, LANES)),
                     jnp.broadcast_to(x[rows:], (rows, LANES)))


def _exp(sign, x):
    return jnp.exp2(x * (sign * LOG2_E))


def _params(sem, vmem_limit=VMEM_LIMIT):
    return pltpu.CompilerParams(dimension_semantics=sem, vmem_limit_bytes=vmem_limit)


def _in_proj_kernel(x_ref, w_ref, wvt_ref, a_ref, vt_ref, b_ref):
    xb = x_ref[...].astype(BF16)
    proj = _dot(xb, w_ref[...])
    wa = a_ref.shape[1]
    a_ref[...] = proj[:, :wa].astype(a_ref.dtype)
    b_ref[...] = proj[:, wa:]
    vt_ref[...] = _dot_nt(wvt_ref[...], xb).astype(vt_ref.dtype)


def _in_proj(x2d, w, wvt, wa, tm):
    n, d = x2d.shape
    wb = w.shape[1] - wa
    wv = wvt.shape[0]
    return pl.pallas_call(
        _in_proj_kernel,
        grid=(n // tm,),
        in_specs=[pl.BlockSpec((tm, d), lambda i: (i, 0)),
                  pl.BlockSpec(w.shape, lambda i: (0, 0)),
                  pl.BlockSpec(wvt.shape, lambda i: (0, 0))],
        out_specs=[pl.BlockSpec((tm, wa), lambda i: (i, 0)),
                   pl.BlockSpec((wv, tm), lambda i: (0, i)),
                   pl.BlockSpec((tm, wb), lambda i: (i, 0))],
        out_shape=[jax.ShapeDtypeStruct((n, wa), BF16), jax.ShapeDtypeStruct((wv, n), BF16),
                   jax.ShapeDtypeStruct((n, wb), F32)],
        compiler_params=_params(("parallel",)),
        name="in_proj",
    )(x2d, w, wvt)


def _sb_kernel(q_ref, k_ref, vt_ref, ntri_ref, g_ref, o_ref, acc_ref, carry_ref, *, tq):
    i = pl.program_id(2)
    pairs = range(q_ref.shape[2] // LANES)
    cols = [slice(t * LANES, (t + 1) * LANES) for t in pairs]
    q2 = [_stack_heads(q_ref[0, :, cols[t]]) for t in pairs]

    acc_ref[...] = jnp.zeros_like(acc_ref)
    carry_ref[...] = jnp.zeros_like(carry_ref)

    def tile(kb, diagonal):
        start = pl.multiple_of(kb * tq, tq)
        if diagonal:
            key = lax.broadcasted_iota(jnp.int32, (tq, 2 * tq), 0)
            col = lax.broadcasted_iota(jnp.int32, (tq, 2 * tq), 1)
            keep = key < jnp.where(col >= tq, col - tq, col)
        k_blks = [k_ref[0, pl.ds(start, tq), cols[t]] for t in pairs]
        vt_blks = [vt_ref[cols[t], pl.ds(start, tq)] for t in pairs]
        ntri = ntri_ref[...]
        acc_old = [acc_ref[t] for t in pairs]
        carry_old = [carry_ref[t] for t in pairs]
        zs = [_dot_nt(k_blks[t], q2[t]) for t in pairs]
        log_betas, afters, carry_new = [], [], []
        for t in pairs:
            z = zs[t]
            stay = jnp.maximum(z, 0.0) + jnp.log(1.0 + _exp(-1.0, jnp.abs(z)))
            log_betas.append(z - stay)
            if diagonal:
                stay = jnp.where(keep, stay, 0.0)
            hi = stay.astype(BF16)
            lo = (stay - hi.astype(F32)).astype(BF16)
            afters.append(_dot(ntri, jnp.concatenate([hi, lo], axis=0)))
            carry_new.append(carry_old[t] - jnp.sum(stay, axis=0, keepdims=True))
        for t in pairs:
            w = _exp(1.0, log_betas[t] + afters[t] + carry_old[t])
            if diagonal:
                w = jnp.where(keep, w, 0.0)
            acc_ref[t] = acc_old[t] + _dot(vt_blks[t], w.astype(BF16))
            carry_ref[t] = carry_new[t]

    tile(i, True)

    def more(state):
        kb, top = state
        return jnp.logical_and(kb >= 0, top > F32_EXP_UNDERFLOW)

    def body(state):
        kb, _ = state
        tile(kb, False)
        return kb - 1, jnp.max(carry_ref[...])

    lax.while_loop(more, body, (i - 1, jnp.zeros((), F32)))

    feature = lax.broadcasted_iota(jnp.int32, (LANES, tq), 0)
    for t in pairs:
        acc = acc_ref[t]
        heads = (acc[:HEAD_DIM, :tq], acc[HEAD_DIM:, tq:])
        scale = [lax.rsqrt(jnp.mean(h * h, axis=0, keepdims=True) + RMS_EPS) for h in heads]
        o = jnp.where(feature < HEAD_DIM, acc[:, :tq] * scale[0], acc[:, tq:] * scale[1])
        o_ref[0, :, cols[t]] = (o.T * g_ref[:, cols[t]]).astype(o_ref.dtype)


def _sb_attention(qk, vt, gain, tq, width):
    b, s, _ = qk.shape
    groups = gain.shape[1] // width
    ntri = -(jnp.arange(tq)[None, :] > jnp.arange(tq)[:, None]).astype(BF16)
    ntri = jnp.concatenate([ntri, ntri], axis=1)
    return pl.pallas_call(
        functools.partial(_sb_kernel, tq=tq),
        grid=(b, groups, s // tq),
        in_specs=[pl.BlockSpec((1, tq, width), lambda bi, p, i: (bi, i, p)),
                  pl.BlockSpec((1, s, width), lambda bi, p, i: (bi, 0, groups + p)),
                  pl.BlockSpec((width, s), lambda bi, p, i: (p, bi)),
                  pl.BlockSpec((tq, 2 * tq), lambda bi, p, i: (0, 0)),
                  pl.BlockSpec((1, width), lambda bi, p, i: (0, p))],
        out_specs=pl.BlockSpec((1, tq, width), lambda bi, p, i: (bi, i, p)),
        out_shape=jax.ShapeDtypeStruct((b, s, groups * width), BF16),
        scratch_shapes=[pltpu.VMEM((width // LANES, LANES, 2 * tq), F32),
                        pltpu.VMEM((width // LANES, 1, 2 * tq), F32)],
        compiler_params=_params(("parallel", "parallel", "arbitrary")),
        name="sb_attn",
    )(qk, qk, vt, ntri, gain)


def _dil_kernel(slopes_ref, q_ref, kp_ref, kc_ref, vp_ref, vc_ref, g_ref, o_ref,
                bias_ref, by4_ref, o4_ref, l4_ref, o16_ref, l16_ref):
    p = pl.program_id(1)
    first = pl.program_id(2) == 0

    row = lax.broadcasted_iota(jnp.int32, (2 * SUB, 2 * SUB), 0)
    col = lax.broadcasted_iota(jnp.int32, (2 * SUB, 2 * SUB), 1)
    dist = jnp.where(row >= SUB, row - SUB, row) + SUB - col
    in_window = (dist >= 0) & (dist <= WINDOW_KEYS)
    slope = jnp.where(row >= SUB, slopes_ref[PAIR * p + 1], slopes_ref[PAIR * p])
    before_start = jnp.where(col < SUB, jnp.where(first, NEG_BIG, 0.0), 0.0)
    for n, d in enumerate(DILATIONS):
        bias = jnp.where(in_window, -slope * (dist * d).astype(F32), NEG_BIG)
        bias_ref[2 * n] = bias
        bias_ref[2 * n + 1] = bias + before_start

    Q, KP, KC, VP, VC = range(5)
    token_order = (q_ref, kp_ref, kc_ref, vp_ref, vc_ref)
    quarter = SUPER // 4
    for a, src in enumerate(token_order):
        for c in range(4):
            by4_ref[a, c * quarter:(c + 1) * quarter, :] = src[0, pl.ds(c, quarter, stride=4), :]

    def rows(a, start, d):
        if d == 1:
            return token_order[a][0, pl.ds(start, SUB), :]
        first_row = (start % 4) * quarter + start // 4
        if d == 4:
            return by4_ref[a, pl.ds(first_row, SUB), :]
        return by4_ref[a, pl.ds(first_row, SUB, stride=d // 4), :]

    def stage(n, start, at_edge):
        d = DILATIONS[n]
        if at_edge:
            k_lo, v_lo = rows(KP, start + SUPER - SUB * d, d), rows(VP, start + SUPER - SUB * d, d)
        else:
            k_lo, v_lo = rows(KC, start - SUB * d, d), rows(VC, start - SUB * d, d)
        kk = jnp.concatenate([k_lo, rows(KC, start, d)], axis=0).astype(BF16)
        vv = jnp.concatenate([v_lo, rows(VC, start, d)], axis=0).astype(BF16)
        return _stack_heads(rows(Q, start, d).astype(BF16)), kk, vv

    def softmax_pv(n, at_edge, qk, vv):
        sc = qk + bias_ref[2 * n + (1 if at_edge else 0)]
        m = jnp.max(sc, axis=-1, keepdims=True)
        e = jnp.exp(sc - m)
        l = jnp.sum(e, axis=-1, keepdims=True)
        o = _dot(e.astype(BF16), vv) / l
        lane = lax.broadcasted_iota(jnp.int32, (SUB, LANES), 1)
        return jnp.where(lane < HEAD_DIM, o[:SUB], o[SUB:]), _unstack_heads(m + jnp.log(l))

    def scatter(n, start, o, lse):
        o_s, l_s = (o4_ref, l4_ref) if n == 1 else (o16_ref, l16_ref)
        o_s[pl.ds(start, SUB, stride=DILATIONS[n]), :] = o
        l_s[pl.ds(start, SUB, stride=DILATIONS[n]), :] = lse

    def merge(n, start, o1, l1):
        span = pl.ds(start, SUB)
        l4, l16 = l4_ref[span, :], l16_ref[span, :]
        top = jnp.maximum(jnp.maximum(l1, l4), l16)
        w1, w4, w16 = jnp.exp(l1 - top), jnp.exp(l4 - top), jnp.exp(l16 - top)
        o = (w1 * o1 + w4 * o4_ref[span, :] + w16 * o16_ref[span, :]) / (w1 + w4 + w16)
        lane = lax.broadcasted_iota(jnp.int32, (SUB, LANES), 1)
        sq = o * o
        ms = [jnp.mean(jnp.where(keep, sq, 0.0), axis=-1, keepdims=True) * PAIR
              for keep in (lane < HEAD_DIM, lane >= HEAD_DIM)]
        scale = jnp.where(lane < HEAD_DIM, lax.rsqrt(ms[0] + RMS_EPS), lax.rsqrt(ms[1] + RMS_EPS))
        o_ref[0, span, :] = (o * scale * g_ref[...]).astype(o_ref.dtype)

    work = [(2, r, True, scatter) for r in range(DILATIONS[2])]
    work += [(1, j * SUB * DILATIONS[1] + r, j == 0, scatter)
             for j in range(SUPER // (SUB * DILATIONS[1])) for r in range(DILATIONS[1])]
    work += [(0, j * SUB, j == 0, merge) for j in range(SUPER // SUB)]

    def front(item):
        q2, kk, vv = stage(*item[:3])
        return _dot_nt(q2, kk), vv

    fronts = [front(item) for item in work[:GROUP]]
    for t, (n, start, edge, sink) in enumerate(work):
        if t + GROUP < len(work):
            fronts.append(front(work[t + GROUP]))
        qk, vv = fronts[t]
        sink(n, start, *softmax_pv(n, edge, qk, vv))


def _dil_attention(qkv, slopes, gain):
    b, s, _ = qkv.shape
    n_pairs = gain.shape[1] // LANES
    cur = lambda c: (lambda bi, p, i, sl: (bi, i, c * n_pairs + p))
    prev = lambda c: (lambda bi, p, i, sl: (bi, jnp.maximum(i - 1, 0), c * n_pairs + p))
    blk = (1, SUPER, LANES)
    work = pltpu.VMEM((SUPER, LANES), F32)
    return pl.pallas_call(
        _dil_kernel,
        grid_spec=pltpu.PrefetchScalarGridSpec(
            num_scalar_prefetch=1,
            grid=(b, n_pairs, s // SUPER),
            in_specs=[pl.BlockSpec(blk, cur(0)),
                      pl.BlockSpec(blk, prev(1)), pl.BlockSpec(blk, cur(1)),
                      pl.BlockSpec(blk, prev(2)), pl.BlockSpec(blk, cur(2)),
                      pl.BlockSpec((1, LANES), lambda bi, p, i, sl: (0, p))],
            out_specs=pl.BlockSpec(blk, cur(0)),
            scratch_shapes=[pltpu.VMEM((2 * len(DILATIONS), 2 * SUB, 2 * SUB), F32),
                            pltpu.VMEM((5, SUPER, LANES), F32), work, work, work, work]),
        out_shape=jax.ShapeDtypeStruct((b, s, n_pairs * LANES), BF16),
        compiler_params=_params(("parallel", "parallel", "arbitrary")),
        name="dil_attn",
    )(slopes, qkv, qkv, qkv, qkv, qkv, gain)


def _post_kernel(x_ref, ma_ref, mb_ref, wo_ref, g1_ref, b1_ref, wg_ref, wu_ref, wd_ref, g2_ref,
                 b2_ref, o_ref, act_ref, *, alpha, chunk, parts):
    half = ma_ref.shape[1]
    rows = x_ref.shape[0] // parts
    spans = [slice(r * rows, (r + 1) * rows) for r in range(parts)]
    mixes = [_dot(ma_ref[sp, :], wo_ref[:half, :]) + _dot(mb_ref[sp, :], wo_ref[half:, :])
             for sp in spans]
    hs = [_layer_norm(alpha * x_ref[sp, :] + mix, g1_ref[...], b1_ref[...])
          for sp, mix in zip(spans, mixes)]
    for r, (sp, h) in enumerate(zip(spans, hs)):
        hb = h.astype(BF16)
        for c in range(act_ref.shape[2] // chunk):
            cols = slice(c * chunk, (c + 1) * chunk)
            gate = _dot(hb, wg_ref[:, cols])
            up = _dot(hb, wu_ref[:, cols])
            act_ref[r, :, cols] = (gate / (1.0 + jnp.exp(-gate)) * up).astype(BF16)
        ffn = _dot(act_ref[r], wd_ref[...])
        o_ref[sp, :] = _layer_norm(alpha * h + ffn, g2_ref[...], b2_ref[...])


def _post(x2d, ma, mb, w_out, ln1_g, ln1_b, w_gate, w_up, w_down, ln2_g, ln2_b, alpha, tm, chunk,
          parts):
    n, d = x2d.shape
    d_ff = w_gate.shape[1]
    row = lambda w: pl.BlockSpec((tm, w), lambda i: (i, 0))
    const = lambda a: pl.BlockSpec(a.shape, lambda i: (0, 0), pipeline_mode=pl.Buffered(1))
    consts = (w_out, ln1_g, ln1_b, w_gate, w_up, w_down, ln2_g, ln2_b)
    return pl.pallas_call(
        functools.partial(_post_kernel, alpha=alpha, chunk=chunk, parts=parts),
        grid=(n // tm,),
        in_specs=[row(d), row(ma.shape[1]), row(mb.shape[1])] + [const(a) for a in consts],
        out_specs=row(d),
        out_shape=jax.ShapeDtypeStruct((n, d), F32),
        scratch_shapes=[pltpu.VMEM((parts, tm // parts, d_ff), BF16)],
        compiler_params=_params(("parallel",), POST_VMEM_LIMIT),
        name="post",
    )(x2d, ma, mb, *consts)


def kernel(x, w_in, g_sb, g_dil, w_out, ln1_g, ln1_b, w_gate, w_up, w_down, ln2_g, ln2_b):
    b, s, d = x.shape
    depth = w_in.shape[0]
    sb_width = g_sb.shape[1]
    dil_width = g_dil.shape[1]
    n_dil_heads = dil_width // HEAD_DIM
    alpha = (2.0 * depth) ** 0.25
    slopes = jnp.exp2(-8.0 * (jnp.arange(n_dil_heads, dtype=F32) + 1.0) / n_dil_heads)
    assert s % SUPER == 0

    h = x.reshape(b * s, d)
    for l in range(depth):
        w = w_in[l].astype(BF16)
        w_tok = jnp.concatenate([w[:, :2 * sb_width], w[:, 3 * sb_width:]], axis=1)
        w_vt = w[:, 2 * sb_width:3 * sb_width].T
        qk_a, vt_a, qkv_b = _in_proj(h, w_tok, w_vt, wa=2 * sb_width, tm=1024)
        mixed_a = _sb_attention(qk_a.reshape(b, s, -1), vt_a, g_sb[l][None, :], tq=256,
                                width=4 * LANES)
        mixed_b = _dil_attention(qkv_b.reshape(b, s, -1), slopes, g_dil[l][None, :])
        h = _post(h, mixed_a.reshape(b * s, sb_width), mixed_b.reshape(b * s, dil_width),
                  w_out[l].astype(BF16), ln1_g[l][None, :], ln1_b[l][None, :],
                  w_gate[l].astype(BF16), w_up[l].astype(BF16), w_down[l].astype(BF16),
                  ln2_g[l][None, :], ln2_b[l][None, :], alpha, tm=1024, chunk=256, parts=2)
    return h.reshape(b, s, d)
```

```python
import functools

import jax
import jax.numpy as jnp
from jax import lax
from jax.experimental import pallas as pl
from jax.experimental.pallas import tpu as pltpu

HEAD_DIM = 64
LANES = 128
PAIR = LANES // HEAD_DIM
DILATIONS = (1, 4, 16)
WINDOW_KEYS = 128
SUB = WINDOW_KEYS
SUPER = SUB * max(DILATIONS)
GROUP = 2
ROW_TILE = 1024
POST_PARTS = 2
FFN_CHUNK = 256
SB_BLOCK = 256
SB_WIDTH = 4 * LANES
LN_EPS = 1e-5
RMS_EPS = 1e-6
NEG_BIG = -1e30
F32_EXP_UNDERFLOW = -104.0
LOG2_E = 1.4426950408889634
VMEM_LIMIT = 48 * 1024 * 1024
POST_VMEM_LIMIT = 58 * 1024 * 1024

F32 = jnp.float32
BF16 = jnp.bfloat16


def _dot(a, b):
    return jnp.dot(a, b, preferred_element_type=F32)


def _dot_nt(a, b):
    return lax.dot_general(a, b, (((1,), (1,)), ((), ())), preferred_element_type=F32)


def _layer_norm(y, g, b):
    mu = jnp.mean(y, axis=-1, keepdims=True)
    d = y - mu
    var = jnp.mean(d * d, axis=-1, keepdims=True)
    return d * lax.rsqrt(var + LN_EPS) * g + b


def _stack_heads(q):
    lane = lax.broadcasted_iota(jnp.int32, q.shape, 1)
    q = q * jnp.asarray(HEAD_DIM ** -0.5, BF16)
    zero = jnp.zeros_like(q)
    return jnp.concatenate([jnp.where(lane < HEAD_DIM, q, zero),
                            jnp.where(lane < HEAD_DIM, zero, q)], axis=0)


def _unstack_heads(x):
    rows = x.shape[0] // PAIR
    lane = lax.broadcasted_iota(jnp.int32, (rows, LANES), 1)
    return jnp.where(lane < HEAD_DIM, jnp.broadcast_to(x[:rows], (rows, LANES)),
                     jnp.broadcast_to(x[rows:], (rows, LANES)))


def _exp(sign, x):
    return jnp.exp2(x * (sign * LOG2_E))


def _params(sem, vmem_limit=VMEM_LIMIT):
    return pltpu.CompilerParams(dimension_semantics=sem, vmem_limit_bytes=vmem_limit)


def _in_proj_kernel(x_ref, wa_ref, wvt_ref, wb_ref, a_ref, vt_ref, b_ref):
    xb = x_ref[...].astype(BF16)
    a_ref[...] = _dot(xb, wa_ref[...]).astype(a_ref.dtype)
    vt_ref[...] = _dot_nt(wvt_ref[...], xb).astype(vt_ref.dtype)
    b_ref[...] = _dot(xb, wb_ref[...])


def _in_proj(x2d, w, wvt, wa, tm):
    n, d = x2d.shape
    wv = wvt.shape[0]
    wb = w.shape[1] - wa - wv
    assert (wa + wv) % wb == 0
    return pl.pallas_call(
        _in_proj_kernel,
        grid=(n // tm,),
        in_specs=[pl.BlockSpec((tm, d), lambda i: (i, 0)),
                  pl.BlockSpec((d, wa), lambda i: (0, 0)),
                  pl.BlockSpec(wvt.shape, lambda i: (0, 0)),
                  pl.BlockSpec((d, wb), lambda i: (0, (wa + wv) // wb))],
        out_specs=[pl.BlockSpec((tm, wa), lambda i: (i, 0)),
                   pl.BlockSpec((wv, tm), lambda i: (0, i)),
                   pl.BlockSpec((tm, wb), lambda i: (i, 0))],
        out_shape=[jax.ShapeDtypeStruct((n, wa), BF16), jax.ShapeDtypeStruct((wv, n), BF16),
                   jax.ShapeDtypeStruct((n, wb), F32)],
        compiler_params=_params(("parallel",)),
        name="in_proj",
    )(x2d, w, wvt, w)


def _sb_kernel(q_ref, k_ref, vt_ref, ntri_ref, g_ref, o_ref, acc_ref, carry_ref, *, tq):
    i = pl.program_id(2)
    pairs = range(q_ref.shape[2] // LANES)
    cols = [slice(t * LANES, (t + 1) * LANES) for t in pairs]
    q2 = [_stack_heads(q_ref[0, :, cols[t]]) for t in pairs]

    acc_ref[...] = jnp.zeros_like(acc_ref)
    carry_ref[...] = jnp.zeros_like(carry_ref)

    def tile(kb, diagonal):
        start = pl.multiple_of(kb * tq, tq)
        if diagonal:
            key = lax.broadcasted_iota(jnp.int32, (tq, 2 * tq), 0)
            col = lax.broadcasted_iota(jnp.int32, (tq, 2 * tq), 1)
            keep = key < jnp.where(col >= tq, col - tq, col)
        k_blks = [k_ref[0, pl.ds(start, tq), cols[t]] for t in pairs]
        vt_blks = [vt_ref[cols[t], pl.ds(start, tq)] for t in pairs]
        ntri = ntri_ref[...]
        acc_old = [acc_ref[t] for t in pairs]
        carry_old = [carry_ref[t] for t in pairs]
        zs = [_dot_nt(k_blks[t], q2[t]) for t in pairs]
        log_betas, afters, carry_new = [], [], []
        for t in pairs:
            z = zs[t]
            stay = jnp.maximum(z, 0.0) + jnp.log(1.0 + _exp(-1.0, jnp.abs(z)))
            log_betas.append(z - stay)
            if diagonal:
                stay = jnp.where(keep, stay, 0.0)
            hi = stay.astype(BF16)
            lo = (stay - hi.astype(F32)).astype(BF16)
            afters.append(_dot(ntri, jnp.concatenate([hi, lo], axis=0)))
            carry_new.append(carry_old[t] - jnp.sum(stay, axis=0, keepdims=True))
        for t in pairs:
            w = _exp(1.0, log_betas[t] + afters[t] + carry_old[t])
            if diagonal:
                w = jnp.where(keep, w, 0.0)
            acc_ref[t] = acc_old[t] + _dot(vt_blks[t], w.astype(BF16))
            carry_ref[t] = carry_new[t]

    tile(i, True)

    def more(state):
        kb, top = state
        return jnp.logical_and(kb >= 0, top > F32_EXP_UNDERFLOW)

    def body(state):
        kb, _ = state
        tile(kb, False)
        return kb - 1, jnp.max(carry_ref[...])

    lax.while_loop(more, body, (i - 1, jnp.zeros((), F32)))

    feature = lax.broadcasted_iota(jnp.int32, (LANES, tq), 0)
    for t in pairs:
        acc = acc_ref[t]
        heads = (acc[:HEAD_DIM, :tq], acc[HEAD_DIM:, tq:])
        scale = [lax.rsqrt(jnp.mean(h * h, axis=0, keepdims=True) + RMS_EPS) for h in heads]
        o = jnp.where(feature < HEAD_DIM, acc[:, :tq] * scale[0], acc[:, tq:] * scale[1])
        o_ref[0, :, cols[t]] = (o.T * g_ref[:, cols[t]]).astype(o_ref.dtype)


def _sb_attention(qk, vt, gain, tq, width):
    b, s, _ = qk.shape
    groups = gain.shape[1] // width
    ntri = -(jnp.arange(tq)[None, :] > jnp.arange(tq)[:, None]).astype(BF16)
    ntri = jnp.concatenate([ntri, ntri], axis=1)
    return pl.pallas_call(
        functools.partial(_sb_kernel, tq=tq),
        grid=(b, groups, s // tq),
        in_specs=[pl.BlockSpec((1, tq, width), lambda bi, p, i: (bi, i, p)),
                  pl.BlockSpec((1, s, width), lambda bi, p, i: (bi, 0, groups + p)),
                  pl.BlockSpec((width, s), lambda bi, p, i: (p, bi)),
                  pl.BlockSpec((tq, 2 * tq), lambda bi, p, i: (0, 0)),
                  pl.BlockSpec((1, width), lambda bi, p, i: (0, p))],
        out_specs=pl.BlockSpec((1, tq, width), lambda bi, p, i: (bi, i, p)),
        out_shape=jax.ShapeDtypeStruct((b, s, groups * width), BF16),
        scratch_shapes=[pltpu.VMEM((width // LANES, LANES, 2 * tq), F32),
                        pltpu.VMEM((width // LANES, 1, 2 * tq), F32)],
        compiler_params=_params(("parallel", "parallel", "arbitrary")),
        name="sb_attn",
    )(qk, qk, vt, ntri, gain)


def _dil_kernel(slopes_ref, q_ref, kp_ref, kc_ref, vp_ref, vc_ref, g_ref, o_ref,
                bias_ref, by4_ref, o4_ref, l4_ref, o16_ref, l16_ref):
    p = pl.program_id(1)
    first = pl.program_id(2) == 0

    row = lax.broadcasted_iota(jnp.int32, (2 * SUB, 2 * SUB), 0)
    col = lax.broadcasted_iota(jnp.int32, (2 * SUB, 2 * SUB), 1)
    dist = jnp.where(row >= SUB, row - SUB, row) + SUB - col
    in_window = (dist >= 0) & (dist <= WINDOW_KEYS)
    slope = jnp.where(row >= SUB, slopes_ref[PAIR * p + 1], slopes_ref[PAIR * p])
    before_start = jnp.where(col < SUB, jnp.where(first, NEG_BIG, 0.0), 0.0)
    for n, d in enumerate(DILATIONS):
        bias = jnp.where(in_window, -slope * (dist * d).astype(F32), NEG_BIG)
        bias_ref[2 * n] = bias
        bias_ref[2 * n + 1] = bias + before_start

    Q, KP, KC, VP, VC = range(5)
    token_order = (q_ref, kp_ref, kc_ref, vp_ref, vc_ref)
    quarter = SUPER // 4
    for a, src in enumerate(token_order):
        for c in range(4):
            by4_ref[a, c * quarter:(c + 1) * quarter, :] = src[0, pl.ds(c, quarter, stride=4), :]

    def rows(a, start, d):
        if d == 1:
            return token_order[a][0, pl.ds(start, SUB), :]
        first_row = (start % 4) * quarter + start // 4
        if d == 4:
            return by4_ref[a, pl.ds(first_row, SUB), :]
        return by4_ref[a, pl.ds(first_row, SUB, stride=d // 4), :]

    def stage(n, start, at_edge):
        d = DILATIONS[n]
        if at_edge:
            k_lo, v_lo = rows(KP, start + SUPER - SUB * d, d), rows(VP, start + SUPER - SUB * d, d)
        else:
            k_lo, v_lo = rows(KC, start - SUB * d, d), rows(VC, start - SUB * d, d)
        kk = jnp.concatenate([k_lo, rows(KC, start, d)], axis=0).astype(BF16)
        vv = jnp.concatenate([v_lo, rows(VC, start, d)], axis=0).astype(BF16)
        return _stack_heads(rows(Q, start, d).astype(BF16)), kk, vv

    def softmax_pv(n, at_edge, qk, vv):
        sc = qk + bias_ref[2 * n + (1 if at_edge else 0)]
        m = jnp.max(sc, axis=-1, keepdims=True)
        e = jnp.exp(sc - m)
        l = jnp.sum(e, axis=-1, keepdims=True)
        o = _dot(e.astype(BF16), vv) / l
        lane = lax.broadcasted_iota(jnp.int32, (SUB, LANES), 1)
        return jnp.where(lane < HEAD_DIM, o[:SUB], o[SUB:]), _unstack_heads(m + jnp.log(l))

    def scatter(n, start, o, lse):
        o_s, l_s = (o4_ref, l4_ref) if n == 1 else (o16_ref, l16_ref)
        o_s[pl.ds(start, SUB, stride=DILATIONS[n]), :] = o
        l_s[pl.ds(start, SUB, stride=DILATIONS[n]), :] = lse

    def merge(n, start, o1, l1):
        span = pl.ds(start, SUB)
        l4, l16 = l4_ref[span, :], l16_ref[span, :]
        top = jnp.maximum(jnp.maximum(l1, l4), l16)
        w1, w4, w16 = jnp.exp(l1 - top), jnp.exp(l4 - top), jnp.exp(l16 - top)
        o = (w1 * o1 + w4 * o4_ref[span, :] + w16 * o16_ref[span, :]) / (w1 + w4 + w16)
        lane = lax.broadcasted_iota(jnp.int32, (SUB, LANES), 1)
        sq = o * o
        ms = [jnp.mean(jnp.where(keep, sq, 0.0), axis=-1, keepdims=True) * PAIR
              for keep in (lane < HEAD_DIM, lane >= HEAD_DIM)]
        scale = jnp.where(lane < HEAD_DIM, lax.rsqrt(ms[0] + RMS_EPS), lax.rsqrt(ms[1] + RMS_EPS))
        o_ref[0, span, :] = (o * scale * g_ref[...]).astype(o_ref.dtype)

    work = [(2, r, True, scatter) for r in range(DILATIONS[2])]
    work += [(1, j * SUB * DILATIONS[1] + r, j == 0, scatter)
             for j in range(SUPER // (SUB * DILATIONS[1])) for r in range(DILATIONS[1])]
    work += [(0, j * SUB, j == 0, merge) for j in range(SUPER // SUB)]

    def front(item):
        q2, kk, vv = stage(*item[:3])
        return _dot_nt(q2, kk), vv

    fronts = [front(item) for item in work[:GROUP]]
    for t, (n, start, edge, sink) in enumerate(work):
        if t + GROUP < len(work):
            fronts.append(front(work[t + GROUP]))
        qk, vv = fronts[t]
        sink(n, start, *softmax_pv(n, edge, qk, vv))


def _dil_attention(qkv, slopes, gain):
    b, s, _ = qkv.shape
    n_pairs = gain.shape[1] // LANES
    cur = lambda c: (lambda bi, p, i, sl: (bi, i, c * n_pairs + p))
    prev = lambda c: (lambda bi, p, i, sl: (bi, jnp.maximum(i - 1, 0), c * n_pairs + p))
    blk = (1, SUPER, LANES)
    work = pltpu.VMEM((SUPER, LANES), F32)
    return pl.pallas_call(
        _dil_kernel,
        grid_spec=pltpu.PrefetchScalarGridSpec(
            num_scalar_prefetch=1,
            grid=(b, n_pairs, s // SUPER),
            in_specs=[pl.BlockSpec(blk, cur(0)),
                      pl.BlockSpec(blk, prev(1)), pl.BlockSpec(blk, cur(1)),
                      pl.BlockSpec(blk, prev(2)), pl.BlockSpec(blk, cur(2)),
                      pl.BlockSpec((1, LANES), lambda bi, p, i, sl: (0, p))],
            out_specs=pl.BlockSpec(blk, cur(0)),
            scratch_shapes=[pltpu.VMEM((2 * len(DILATIONS), 2 * SUB, 2 * SUB), F32),
                            pltpu.VMEM((5, SUPER, LANES), F32), work, work, work, work]),
        out_shape=jax.ShapeDtypeStruct((b, s, n_pairs * LANES), BF16),
        compiler_params=_params(("parallel", "parallel", "arbitrary")),
        name="dil_attn",
    )(slopes, qkv, qkv, qkv, qkv, qkv, gain)


def _post_kernel(x_ref, ma_ref, mb_ref, wo_ref, g1_ref, b1_ref, wg_ref, wu_ref, wd_ref, g2_ref,
                 b2_ref, o_ref, act_ref, *, alpha, chunk, parts):
    rows = x_ref.shape[0] // parts
    spans = [slice(r * rows, (r + 1) * rows) for r in range(parts)]
    mixes = [_dot(jnp.concatenate([ma_ref[sp, :], mb_ref[sp, :]], axis=1), wo_ref[...])
             for sp in spans]
    hs = [_layer_norm(alpha * x_ref[sp, :] + mix, g1_ref[...], b1_ref[...])
          for sp, mix in zip(spans, mixes)]
    for r, (sp, h) in enumerate(zip(spans, hs)):
        hb = h.astype(BF16)
        for c in range(act_ref.shape[2] // chunk):
            cols = slice(c * chunk, (c + 1) * chunk)
            gate = _dot(hb, wg_ref[:, cols])
            up = _dot(hb, wu_ref[:, cols])
            act_ref[r, :, cols] = (gate / (1.0 + jnp.exp(-gate)) * up).astype(BF16)
        ffn = _dot(act_ref[r], wd_ref[...])
        o_ref[sp, :] = _layer_norm(alpha * h + ffn, g2_ref[...], b2_ref[...])


def _post(x2d, ma, mb, w_out, ln1_g, ln1_b, w_gate, w_up, w_down, ln2_g, ln2_b, alpha, tm, chunk,
          parts):
    n, d = x2d.shape
    d_ff = w_gate.shape[1]
    row = lambda w: pl.BlockSpec((tm, w), lambda i: (i, 0))
    const = lambda a: pl.BlockSpec(a.shape, lambda i: (0, 0), pipeline_mode=pl.Buffered(1))
    consts = (w_out, ln1_g, ln1_b, w_gate, w_up, w_down, ln2_g, ln2_b)
    return pl.pallas_call(
        functools.partial(_post_kernel, alpha=alpha, chunk=chunk, parts=parts),
        grid=(n // tm,),
        in_specs=[row(d), row(ma.shape[1]), row(mb.shape[1])] + [const(a) for a in consts],
        out_specs=row(d),
        out_shape=jax.ShapeDtypeStruct((n, d), F32),
        scratch_shapes=[pltpu.VMEM((parts, tm // parts, d_ff), BF16)],
        compiler_params=_params(("parallel",), POST_VMEM_LIMIT),
        name="post",
    )(x2d, ma, mb, *consts)


def kernel(x, w_in, g_sb, g_dil, w_out, ln1_g, ln1_b, w_gate, w_up, w_down, ln2_g, ln2_b):
    b, s, d = x.shape
    depth = w_in.shape[0]
    sb_width = g_sb.shape[1]
    dil_width = g_dil.shape[1]
    n_dil_heads = dil_width // HEAD_DIM
    alpha = (2.0 * depth) ** 0.25
    slopes = jnp.exp2(-8.0 * (jnp.arange(n_dil_heads, dtype=F32) + 1.0) / n_dil_heads)
    assert s % SUPER == 0

    h = x.reshape(b * s, d)
    for l in range(depth):
        w = w_in[l].astype(BF16)
        w_vt = w[:, 2 * sb_width:3 * sb_width].T
        qk_a, vt_a, qkv_b = _in_proj(h, w, w_vt, wa=2 * sb_width, tm=ROW_TILE)
        mixed_a = _sb_attention(qk_a.reshape(b, s, -1), vt_a, g_sb[l][None, :], tq=SB_BLOCK,
                                width=SB_WIDTH)
        mixed_b = _dil_attention(qkv_b.reshape(b, s, -1), slopes, g_dil[l][None, :])
        h = _post(h, mixed_a.reshape(b * s, sb_width), mixed_b.reshape(b * s, dil_width),
                  w_out[l].astype(BF16), ln1_g[l][None, :], ln1_b[l][None, :],
                  w_gate[l].astype(BF16), w_up[l].astype(BF16), w_down[l].astype(BF16),
                  ln2_g[l][None, :], ln2_b[l][None, :], alpha, tm=ROW_TILE, chunk=FFN_CHUNK,
                  parts=POST_PARTS)
    return h.reshape(b, s, d)
```

```python
import functools

import jax
import jax.numpy as jnp
from jax import lax
from jax.experimental import pallas as pl
from jax.experimental.pallas import tpu as pltpu

HEAD_DIM = 64
LANES = 128
PAIR = LANES // HEAD_DIM
DILATIONS = (1, 4, 16)
WINDOW_KEYS = 128
SUB = WINDOW_KEYS
SUPER = SUB * max(DILATIONS)
GROUP = 2
ROW_TILE = 1024
POST_PARTS = 2
FFN_CHUNK = 256
SB_BLOCK = 256
SB_WIDTH = 4 * LANES
SB_STEP_BLOCKS = 4
LN_EPS = 1e-5
RMS_EPS = 1e-6
NEG_BIG = -1e30
F32_EXP_UNDERFLOW = -104.0
LOG2_E = 1.4426950408889634
VMEM_LIMIT = 48 * 1024 * 1024
POST_VMEM_LIMIT = 58 * 1024 * 1024

F32 = jnp.float32
BF16 = jnp.bfloat16


def _dot(a, b):
    return jnp.dot(a, b, preferred_element_type=F32)


def _dot_nt(a, b):
    return lax.dot_general(a, b, (((1,), (1,)), ((), ())), preferred_element_type=F32)


def _layer_norm(y, g, b):
    mu = jnp.mean(y, axis=-1, keepdims=True)
    d = y - mu
    var = jnp.mean(d * d, axis=-1, keepdims=True)
    return d * lax.rsqrt(var + LN_EPS) * g + b


def _stack_heads(q):
    lane = lax.broadcasted_iota(jnp.int32, q.shape, 1)
    q = q * jnp.asarray(HEAD_DIM ** -0.5, BF16)
    zero = jnp.zeros_like(q)
    return jnp.concatenate([jnp.where(lane < HEAD_DIM, q, zero),
                            jnp.where(lane < HEAD_DIM, zero, q)], axis=0)


def _unstack_heads(x):
    rows = x.shape[0] // PAIR
    lane = lax.broadcasted_iota(jnp.int32, (rows, LANES), 1)
    return jnp.where(lane < HEAD_DIM, jnp.broadcast_to(x[:rows], (rows, LANES)),
                     jnp.broadcast_to(x[rows:], (rows, LANES)))


def _exp(sign, x):
    return jnp.exp2(x * (sign * LOG2_E))


def _params(sem, vmem_limit=VMEM_LIMIT):
    return pltpu.CompilerParams(dimension_semantics=sem, vmem_limit_bytes=vmem_limit)


def _in_proj_kernel(x_ref, wa_ref, wvt_ref, wb_ref, a_ref, vt_ref, b_ref):
    xb = x_ref[...].astype(BF16)
    a_ref[...] = _dot(xb, wa_ref[...]).astype(a_ref.dtype)
    vt_ref[...] = _dot_nt(wvt_ref[...], xb).astype(vt_ref.dtype)
    b_ref[...] = _dot(xb, wb_ref[...])


def _in_proj(x2d, w, wvt, wa, tm):
    n, d = x2d.shape
    wv = wvt.shape[0]
    wb = w.shape[1] - wa - wv
    assert (wa + wv) % wb == 0
    return pl.pallas_call(
        _in_proj_kernel,
        grid=(n // tm,),
        in_specs=[pl.BlockSpec((tm, d), lambda i: (i, 0)),
                  pl.BlockSpec((d, wa), lambda i: (0, 0)),
                  pl.BlockSpec(wvt.shape, lambda i: (0, 0)),
                  pl.BlockSpec((d, wb), lambda i: (0, (wa + wv) // wb))],
        out_specs=[pl.BlockSpec((tm, wa), lambda i: (i, 0)),
                   pl.BlockSpec((wv, tm), lambda i: (0, i)),
                   pl.BlockSpec((tm, wb), lambda i: (i, 0))],
        out_shape=[jax.ShapeDtypeStruct((n, wa), BF16), jax.ShapeDtypeStruct((wv, n), BF16),
                   jax.ShapeDtypeStruct((n, wb), F32)],
        compiler_params=_params(("parallel",)),
        name="in_proj",
    )(x2d, w, wvt, w)


def _sb_kernel(q_ref, k_ref, vt_ref, ntri_ref, g_ref, o_ref, acc_ref, carry_ref, *, tq):
    pairs = range(q_ref.shape[2] // LANES)
    cols = [slice(t * LANES, (t + 1) * LANES) for t in pairs]
    blocks = q_ref.shape[1] // tq
    for qb in range(blocks):
        _sb_block(pl.program_id(2) * blocks + qb, slice(qb * tq, (qb + 1) * tq), pairs, cols,
                  q_ref, k_ref, vt_ref, ntri_ref, g_ref, o_ref, acc_ref, carry_ref, tq)


def _sb_block(i, span, pairs, cols, q_ref, k_ref, vt_ref, ntri_ref, g_ref, o_ref, acc_ref,
              carry_ref, tq):
    q2 = [_stack_heads(q_ref[0, span, cols[t]]) for t in pairs]

    acc_ref[...] = jnp.zeros_like(acc_ref)
    carry_ref[...] = jnp.zeros_like(carry_ref)

    def tile(kb, diagonal):
        start = pl.multiple_of(kb * tq, tq)
        if diagonal:
            key = lax.broadcasted_iota(jnp.int32, (tq, 2 * tq), 0)
            col = lax.broadcasted_iota(jnp.int32, (tq, 2 * tq), 1)
            keep = key < jnp.where(col >= tq, col - tq, col)
        k_blks = [k_ref[0, pl.ds(start, tq), cols[t]] for t in pairs]
        vt_blks = [vt_ref[cols[t], pl.ds(start, tq)] for t in pairs]
        ntri = ntri_ref[...]
        acc_old = [acc_ref[t] for t in pairs]
        carry_old = [carry_ref[t] for t in pairs]
        zs = [_dot_nt(k_blks[t], q2[t]) for t in pairs]
        log_betas, afters, carry_new = [], [], []
        for t in pairs:
            z = zs[t]
            stay = jnp.maximum(z, 0.0) + jnp.log(1.0 + _exp(-1.0, jnp.abs(z)))
            log_betas.append(z - stay)
            if diagonal:
                stay = jnp.where(keep, stay, 0.0)
            hi = stay.astype(BF16)
            lo = (stay - hi.astype(F32)).astype(BF16)
            afters.append(_dot(ntri, jnp.concatenate([hi, lo], axis=0)))
            carry_new.append(carry_old[t] - jnp.sum(stay, axis=0, keepdims=True))
        for t in pairs:
            w = _exp(1.0, log_betas[t] + afters[t] + carry_old[t])
            if diagonal:
                w = jnp.where(keep, w, 0.0)
            acc_ref[t] = acc_old[t] + _dot(vt_blks[t], w.astype(BF16))
            carry_ref[t] = carry_new[t]

    tile(i, True)

    def more(state):
        kb, top = state
        return jnp.logical_and(kb >= 0, top > F32_EXP_UNDERFLOW)

    def body(state):
        kb, _ = state
        tile(kb, False)
        return kb - 1, jnp.max(carry_ref[...])

    lax.while_loop(more, body, (i - 1, jnp.zeros((), F32)))

    feature = lax.broadcasted_iota(jnp.int32, (LANES, tq), 0)
    for t in pairs:
        acc = acc_ref[t]
        heads = (acc[:HEAD_DIM, :tq], acc[HEAD_DIM:, tq:])
        scale = [lax.rsqrt(jnp.mean(h * h, axis=0, keepdims=True) + RMS_EPS) for h in heads]
        o = jnp.where(feature < HEAD_DIM, acc[:, :tq] * scale[0], acc[:, tq:] * scale[1])
        o_ref[0, span, cols[t]] = (o.T * g_ref[:, cols[t]]).astype(o_ref.dtype)


def _sb_attention(qk, vt, gain, tq, width, blocks):
    b, s, _ = qk.shape
    groups = gain.shape[1] // width
    ntri = -(jnp.arange(tq)[None, :] > jnp.arange(tq)[:, None]).astype(BF16)
    ntri = jnp.concatenate([ntri, ntri], axis=1)
    return pl.pallas_call(
        functools.partial(_sb_kernel, tq=tq),
        grid=(b, groups, s // (blocks * tq)),
        in_specs=[pl.BlockSpec((1, blocks * tq, width), lambda bi, p, i: (bi, i, p)),
                  pl.BlockSpec((1, s, width), lambda bi, p, i: (bi, 0, groups + p)),
                  pl.BlockSpec((width, s), lambda bi, p, i: (p, bi)),
                  pl.BlockSpec((tq, 2 * tq), lambda bi, p, i: (0, 0)),
                  pl.BlockSpec((1, width), lambda bi, p, i: (0, p))],
        out_specs=pl.BlockSpec((1, blocks * tq, width), lambda bi, p, i: (bi, i, p)),
        out_shape=jax.ShapeDtypeStruct((b, s, groups * width), BF16),
        scratch_shapes=[pltpu.VMEM((width // LANES, LANES, 2 * tq), F32),
                        pltpu.VMEM((width // LANES, 1, 2 * tq), F32)],
        compiler_params=_params(("parallel", "parallel", "arbitrary")),
        name="sb_attn",
    )(qk, qk, vt, ntri, gain)


def _dil_kernel(slopes_ref, q_ref, kp_ref, kc_ref, vp_ref, vc_ref, g_ref, o_ref,
                bias_ref, by4_ref, o4_ref, l4_ref, o16_ref, l16_ref):
    p = pl.program_id(1)
    first = pl.program_id(2) == 0

    row = lax.broadcasted_iota(jnp.int32, (2 * SUB, 2 * SUB), 0)
    col = lax.broadcasted_iota(jnp.int32, (2 * SUB, 2 * SUB), 1)
    dist = jnp.where(row >= SUB, row - SUB, row) + SUB - col
    in_window = (dist >= 0) & (dist <= WINDOW_KEYS)
    slope = jnp.where(row >= SUB, slopes_ref[PAIR * p + 1], slopes_ref[PAIR * p])
    before_start = jnp.where(col < SUB, jnp.where(first, NEG_BIG, 0.0), 0.0)
    for n, d in enumerate(DILATIONS):
        bias = jnp.where(in_window, -slope * (dist * d).astype(F32), NEG_BIG)
        bias_ref[2 * n] = bias
        bias_ref[2 * n + 1] = bias + before_start

    Q, KP, KC, VP, VC = range(5)
    token_order = (q_ref, kp_ref, kc_ref, vp_ref, vc_ref)
    quarter = SUPER // 4
    for a, src in enumerate(token_order):
        for c in range(4):
            by4_ref[a, c * quarter:(c + 1) * quarter, :] = src[0, pl.ds(c, quarter, stride=4), :]

    def rows(a, start, d):
        if d == 1:
            return token_order[a][0, pl.ds(start, SUB), :]
        first_row = (start % 4) * quarter + start // 4
        if d == 4:
            return by4_ref[a, pl.ds(first_row, SUB), :]
        return by4_ref[a, pl.ds(first_row, SUB, stride=d // 4), :]

    def stage(n, start, at_edge):
        d = DILATIONS[n]
        if at_edge:
            k_lo, v_lo = rows(KP, start + SUPER - SUB * d, d), rows(VP, start + SUPER - SUB * d, d)
        else:
            k_lo, v_lo = rows(KC, start - SUB * d, d), rows(VC, start - SUB * d, d)
        kk = jnp.concatenate([k_lo, rows(KC, start, d)], axis=0).astype(BF16)
        vv = jnp.concatenate([v_lo, rows(VC, start, d)], axis=0).astype(BF16)
        return _stack_heads(rows(Q, start, d).astype(BF16)), kk, vv

    def softmax_pv(n, at_edge, qk, vv):
        sc = qk + bias_ref[2 * n + (1 if at_edge else 0)]
        m = jnp.max(sc, axis=-1, keepdims=True)
        e = jnp.exp(sc - m)
        l = jnp.sum(e, axis=-1, keepdims=True)
        o = _dot(e.astype(BF16), vv) / l
        lane = lax.broadcasted_iota(jnp.int32, (SUB, LANES), 1)
        return jnp.where(lane < HEAD_DIM, o[:SUB], o[SUB:]), _unstack_heads(m + jnp.log(l))

    def scatter(n, start, o, lse):
        o_s, l_s = (o4_ref, l4_ref) if n == 1 else (o16_ref, l16_ref)
        o_s[pl.ds(start, SUB, stride=DILATIONS[n]), :] = o
        l_s[pl.ds(start, SUB, stride=DILATIONS[n]), :] = lse

    def merge(n, start, o1, l1):
        span = pl.ds(start, SUB)
        l4, l16 = l4_ref[span, :], l16_ref[span, :]
        top = jnp.maximum(jnp.maximum(l1, l4), l16)
        w1, w4, w16 = jnp.exp(l1 - top), jnp.exp(l4 - top), jnp.exp(l16 - top)
        o = (w1 * o1 + w4 * o4_ref[span, :] + w16 * o16_ref[span, :]) / (w1 + w4 + w16)
        lane = lax.broadcasted_iota(jnp.int32, (SUB, LANES), 1)
        sq = o * o
        ms = [jnp.mean(jnp.where(keep, sq, 0.0), axis=-1, keepdims=True) * PAIR
              for keep in (lane < HEAD_DIM, lane >= HEAD_DIM)]
        scale = jnp.where(lane < HEAD_DIM, lax.rsqrt(ms[0] + RMS_EPS), lax.rsqrt(ms[1] + RMS_EPS))
        o_ref[0, span, :] = (o * scale * g_ref[...]).astype(o_ref.dtype)

    work = [(2, r, True, scatter) for r in range(DILATIONS[2])]
    work += [(1, j * SUB * DILATIONS[1] + r, j == 0, scatter)
             for j in range(SUPER // (SUB * DILATIONS[1])) for r in range(DILATIONS[1])]
    work += [(0, j * SUB, j == 0, merge) for j in range(SUPER // SUB)]

    def front(item):
        q2, kk, vv = stage(*item[:3])
        return _dot_nt(q2, kk), vv

    fronts = [front(item) for item in work[:GROUP]]
    for t, (n, start, edge, sink) in enumerate(work):
        if t + GROUP < len(work):
            fronts.append(front(work[t + GROUP]))
        qk, vv = fronts[t]
        sink(n, start, *softmax_pv(n, edge, qk, vv))


def _dil_attention(qkv, slopes, gain):
    b, s, _ = qkv.shape
    n_pairs = gain.shape[1] // LANES
    cur = lambda c: (lambda bi, p, i, sl: (bi, i, c * n_pairs + p))
    prev = lambda c: (lambda bi, p, i, sl: (bi, jnp.maximum(i - 1, 0), c * n_pairs + p))
    blk = (1, SUPER, LANES)
    work = pltpu.VMEM((SUPER, LANES), F32)
    return pl.pallas_call(
        _dil_kernel,
        grid_spec=pltpu.PrefetchScalarGridSpec(
            num_scalar_prefetch=1,
            grid=(b, n_pairs, s // SUPER),
            in_specs=[pl.BlockSpec(blk, cur(0)),
                      pl.BlockSpec(blk, prev(1)), pl.BlockSpec(blk, cur(1)),
                      pl.BlockSpec(blk, prev(2)), pl.BlockSpec(blk, cur(2)),
                      pl.BlockSpec((1, LANES), lambda bi, p, i, sl: (0, p))],
            out_specs=pl.BlockSpec(blk, cur(0)),
            scratch_shapes=[pltpu.VMEM((2 * len(DILATIONS), 2 * SUB, 2 * SUB), F32),
                            pltpu.VMEM((5, SUPER, LANES), F32), work, work, work, work]),
        out_shape=jax.ShapeDtypeStruct((b, s, n_pairs * LANES), BF16),
        compiler_params=_params(("parallel", "parallel", "arbitrary")),
        name="dil_attn",
    )(slopes, qkv, qkv, qkv, qkv, qkv, gain)


def _post_kernel(x_ref, ma_ref, mb_ref, wo_ref, g1_ref, b1_ref, wg_ref, wu_ref, wd_ref, g2_ref,
                 b2_ref, o_ref, act_ref, *, alpha, chunk, parts):
    rows = x_ref.shape[0] // parts
    spans = [slice(r * rows, (r + 1) * rows) for r in range(parts)]
    mixes = [_dot(jnp.concatenate([ma_ref[sp, :], mb_ref[sp, :]], axis=1), wo_ref[...])
             for sp in spans]
    hs = [_layer_norm(alpha * x_ref[sp, :] + mix, g1_ref[...], b1_ref[...])
          for sp, mix in zip(spans, mixes)]
    for r, (sp, h) in enumerate(zip(spans, hs)):
        hb = h.astype(BF16)
        for c in range(act_ref.shape[2] // chunk):
            cols = slice(c * chunk, (c + 1) * chunk)
            gate = _dot(hb, wg_ref[:, cols])
            up = _dot(hb, wu_ref[:, cols])
            act_ref[r, :, cols] = (gate / (1.0 + jnp.exp(-gate)) * up).astype(BF16)
        ffn = _dot(act_ref[r], wd_ref[...])
        o_ref[sp, :] = _layer_norm(alpha * h + ffn, g2_ref[...], b2_ref[...])


def _post(x2d, ma, mb, w_out, ln1_g, ln1_b, w_gate, w_up, w_down, ln2_g, ln2_b, alpha, tm, chunk,
          parts):
    n, d = x2d.shape
    d_ff = w_gate.shape[1]
    row = lambda w: pl.BlockSpec((tm, w), lambda i: (i, 0))
    const = lambda a: pl.BlockSpec(a.shape, lambda i: (0, 0), pipeline_mode=pl.Buffered(1))
    consts = (w_out, ln1_g, ln1_b, w_gate, w_up, w_down, ln2_g, ln2_b)
    return pl.pallas_call(
        functools.partial(_post_kernel, alpha=alpha, chunk=chunk, parts=parts),
        grid=(n // tm,),
        in_specs=[row(d), row(ma.shape[1]), row(mb.shape[1])] + [const(a) for a in consts],
        out_specs=row(d),
        out_shape=jax.ShapeDtypeStruct((n, d), F32),
        scratch_shapes=[pltpu.VMEM((parts, tm // parts, d_ff), BF16)],
        compiler_params=_params(("parallel",), POST_VMEM_LIMIT),
        name="post",
    )(x2d, ma, mb, *consts)


def kernel(x, w_in, g_sb, g_dil, w_out, ln1_g, ln1_b, w_gate, w_up, w_down, ln2_g, ln2_b):
    b, s, d = x.shape
    depth = w_in.shape[0]
    sb_width = g_sb.shape[1]
    dil_width = g_dil.shape[1]
    n_dil_heads = dil_width // HEAD_DIM
    alpha = (2.0 * depth) ** 0.25
    slopes = jnp.exp2(-8.0 * (jnp.arange(n_dil_heads, dtype=F32) + 1.0) / n_dil_heads)
    assert s % SUPER == 0

    h = x.reshape(b * s, d)
    for l in range(depth):
        w = w_in[l].astype(BF16)
        w_vt = w[:, 2 * sb_width:3 * sb_width].T
        qk_a, vt_a, qkv_b = _in_proj(h, w, w_vt, wa=2 * sb_width, tm=ROW_TILE)
        mixed_a = _sb_attention(qk_a.reshape(b, s, -1), vt_a, g_sb[l][None, :], tq=SB_BLOCK,
                                width=SB_WIDTH, blocks=SB_STEP_BLOCKS)
        mixed_b = _dil_attention(qkv_b.reshape(b, s, -1), slopes, g_dil[l][None, :])
        h = _post(h, mixed_a.reshape(b * s, sb_width), mixed_b.reshape(b * s, dil_width),
                  w_out[l].astype(BF16), ln1_g[l][None, :], ln1_b[l][None, :],
                  w_gate[l].astype(BF16), w_up[l].astype(BF16), w_down[l].astype(BF16),
                  ln2_g[l][None, :], ln2_b[l][None, :], alpha, tm=ROW_TILE, chunk=FFN_CHUNK,
                  parts=POST_PARTS)
    return h.reshape(b, s, d)
```

```python
import functools

import jax
import jax.numpy as jnp
from jax import lax
from jax.experimental import pallas as pl
from jax.experimental.pallas import tpu as pltpu

HEAD_DIM = 64
LANES = 128
PAIR = LANES // HEAD_DIM
DILATIONS = (1, 4, 16)
WINDOW_KEYS = 128
SUB = WINDOW_KEYS
SUPER = SUB * max(DILATIONS)
GROUP = 2
ROW_TILE = 1024
POST_PARTS = 4
FFN_CHUNK = 256
SB_BLOCK = 256
SB_WIDTH = 4 * LANES
SB_STEP_BLOCKS = 4
LN_EPS = 1e-5
RMS_EPS = 1e-6
NEG_BIG = -1e30
F32_EXP_UNDERFLOW = -104.0
LOG2_E = 1.4426950408889634
VMEM_LIMIT = 48 * 1024 * 1024
POST_VMEM_LIMIT = 58 * 1024 * 1024

F32 = jnp.float32
BF16 = jnp.bfloat16


def _dot(a, b):
    return jnp.dot(a, b, preferred_element_type=F32)


def _dot_nt(a, b):
    return lax.dot_general(a, b, (((1,), (1,)), ((), ())), preferred_element_type=F32)


def _layer_norm(y, g, b):
    mu = jnp.mean(y, axis=-1, keepdims=True)
    d = y - mu
    var = jnp.mean(d * d, axis=-1, keepdims=True)
    return d * lax.rsqrt(var + LN_EPS) * g + b


def _stack_heads(q):
    lane = lax.broadcasted_iota(jnp.int32, q.shape, 1)
    q = q * jnp.asarray(HEAD_DIM ** -0.5, BF16)
    zero = jnp.zeros_like(q)
    return jnp.concatenate([jnp.where(lane < HEAD_DIM, q, zero),
                            jnp.where(lane < HEAD_DIM, zero, q)], axis=0)


def _unstack_heads(x):
    rows = x.shape[0] // PAIR
    lane = lax.broadcasted_iota(jnp.int32, (rows, LANES), 1)
    return jnp.where(lane < HEAD_DIM, jnp.broadcast_to(x[:rows], (rows, LANES)),
                     jnp.broadcast_to(x[rows:], (rows, LANES)))


def _exp(sign, x):
    return jnp.exp2(x * (sign * LOG2_E))


def _params(sem, vmem_limit=VMEM_LIMIT):
    return pltpu.CompilerParams(dimension_semantics=sem, vmem_limit_bytes=vmem_limit)


def _in_proj_kernel(x_ref, wa_ref, wvt_ref, wb_ref, a_ref, vt_ref, b_ref):
    xb = x_ref[...].astype(BF16)
    a_ref[...] = _dot(xb, wa_ref[...]).astype(a_ref.dtype)
    vt_ref[...] = _dot_nt(wvt_ref[...], xb).astype(vt_ref.dtype)
    b_ref[...] = _dot(xb, wb_ref[...])


def _in_proj(x2d, w, wvt, wa, tm):
    n, d = x2d.shape
    wv = wvt.shape[0]
    wb = w.shape[1] - wa - wv
    assert (wa + wv) % wb == 0
    return pl.pallas_call(
        _in_proj_kernel,
        grid=(n // tm,),
        in_specs=[pl.BlockSpec((tm, d), lambda i: (i, 0)),
                  pl.BlockSpec((d, wa), lambda i: (0, 0)),
                  pl.BlockSpec(wvt.shape, lambda i: (0, 0)),
                  pl.BlockSpec((d, wb), lambda i: (0, (wa + wv) // wb))],
        out_specs=[pl.BlockSpec((tm, wa), lambda i: (i, 0)),
                   pl.BlockSpec((wv, tm), lambda i: (0, i)),
                   pl.BlockSpec((tm, wb), lambda i: (i, 0))],
        out_shape=[jax.ShapeDtypeStruct((n, wa), BF16), jax.ShapeDtypeStruct((wv, n), BF16),
                   jax.ShapeDtypeStruct((n, wb), F32)],
        compiler_params=_params(("parallel",)),
        name="in_proj",
    )(x2d, w, wvt, w)


def _sb_kernel(q_ref, k_ref, vt_ref, ntri_ref, g_ref, o_ref, acc_ref, carry_ref, *, tq):
    pairs = range(q_ref.shape[2] // LANES)
    cols = [slice(t * LANES, (t + 1) * LANES) for t in pairs]
    blocks = q_ref.shape[1] // tq
    for qb in range(blocks):
        _sb_block(pl.program_id(2) * blocks + qb, slice(qb * tq, (qb + 1) * tq), pairs, cols,
                  q_ref, k_ref, vt_ref, ntri_ref, g_ref, o_ref, acc_ref, carry_ref, tq)


def _sb_block(i, span, pairs, cols, q_ref, k_ref, vt_ref, ntri_ref, g_ref, o_ref, acc_ref,
              carry_ref, tq):
    q2 = [_stack_heads(q_ref[0, span, cols[t]]) for t in pairs]

    acc_ref[...] = jnp.zeros_like(acc_ref)
    carry_ref[...] = jnp.zeros_like(carry_ref)

    def tile(kb, diagonal):
        start = pl.multiple_of(kb * tq, tq)
        if diagonal:
            key = lax.broadcasted_iota(jnp.int32, (tq, 2 * tq), 0)
            col = lax.broadcasted_iota(jnp.int32, (tq, 2 * tq), 1)
            keep = key < jnp.where(col >= tq, col - tq, col)
        k_blks = [k_ref[0, pl.ds(start, tq), cols[t]] for t in pairs]
        vt_blks = [vt_ref[cols[t], pl.ds(start, tq)] for t in pairs]
        ntri = ntri_ref[...]
        acc_old = [acc_ref[t] for t in pairs]
        carry_old = [carry_ref[t] for t in pairs]
        zs = [_dot_nt(k_blks[t], q2[t]) for t in pairs]
        log_betas, afters, carry_new = [], [], []
        for t in pairs:
            z = zs[t]
            stay = jnp.maximum(z, 0.0) + jnp.log(1.0 + _exp(-1.0, jnp.abs(z)))
            log_betas.append(z - stay)
            if diagonal:
                stay = jnp.where(keep, stay, 0.0)
            hi = stay.astype(BF16)
            lo = (stay - hi.astype(F32)).astype(BF16)
            afters.append(_dot(ntri, jnp.concatenate([hi, lo], axis=0)))
            carry_new.append(carry_old[t] - jnp.sum(stay, axis=0, keepdims=True))
        for t in pairs:
            w = _exp(1.0, log_betas[t] + afters[t] + carry_old[t])
            if diagonal:
                w = jnp.where(keep, w, 0.0)
            acc_ref[t] = acc_old[t] + _dot(vt_blks[t], w.astype(BF16))
            carry_ref[t] = carry_new[t]

    tile(i, True)

    def more(state):
        kb, top = state
        return jnp.logical_and(kb >= 0, top > F32_EXP_UNDERFLOW)

    def body(state):
        kb, _ = state
        tile(kb, False)
        return kb - 1, jnp.max(carry_ref[...])

    lax.while_loop(more, body, (i - 1, jnp.zeros((), F32)))

    feature = lax.broadcasted_iota(jnp.int32, (LANES, tq), 0)
    for t in pairs:
        acc = acc_ref[t]
        heads = (acc[:HEAD_DIM, :tq], acc[HEAD_DIM:, tq:])
        scale = [lax.rsqrt(jnp.mean(h * h, axis=0, keepdims=True) + RMS_EPS) for h in heads]
        o = jnp.where(feature < HEAD_DIM, acc[:, :tq] * scale[0], acc[:, tq:] * scale[1])
        o_ref[0, span, cols[t]] = (o.T * g_ref[:, cols[t]]).astype(o_ref.dtype)


def _sb_attention(qk, vt, gain, tq, width, blocks):
    b, s, _ = qk.shape
    groups = gain.shape[1] // width
    ntri = -(jnp.arange(tq)[None, :] > jnp.arange(tq)[:, None]).astype(BF16)
    ntri = jnp.concatenate([ntri, ntri], axis=1)
    return pl.pallas_call(
        functools.partial(_sb_kernel, tq=tq),
        grid=(b, groups, s // (blocks * tq)),
        in_specs=[pl.BlockSpec((1, blocks * tq, width), lambda bi, p, i: (bi, i, p)),
                  pl.BlockSpec((1, s, width), lambda bi, p, i: (bi, 0, groups + p)),
                  pl.BlockSpec((width, s), lambda bi, p, i: (p, bi)),
                  pl.BlockSpec((tq, 2 * tq), lambda bi, p, i: (0, 0)),
                  pl.BlockSpec((1, width), lambda bi, p, i: (0, p))],
        out_specs=pl.BlockSpec((1, blocks * tq, width), lambda bi, p, i: (bi, i, p)),
        out_shape=jax.ShapeDtypeStruct((b, s, groups * width), BF16),
        scratch_shapes=[pltpu.VMEM((width // LANES, LANES, 2 * tq), F32),
                        pltpu.VMEM((width // LANES, 1, 2 * tq), F32)],
        compiler_params=_params(("parallel", "parallel", "arbitrary")),
        name="sb_attn",
    )(qk, qk, vt, ntri, gain)


def _dil_kernel(slopes_ref, q_ref, kp_ref, kc_ref, vp_ref, vc_ref, g_ref, o_ref,
                bias_ref, by4_ref, o4_ref, l4_ref, o16_ref, l16_ref):
    p = pl.program_id(1)
    first = pl.program_id(2) == 0

    row = lax.broadcasted_iota(jnp.int32, (2 * SUB, 2 * SUB), 0)
    col = lax.broadcasted_iota(jnp.int32, (2 * SUB, 2 * SUB), 1)
    dist = jnp.where(row >= SUB, row - SUB, row) + SUB - col
    in_window = (dist >= 0) & (dist <= WINDOW_KEYS)
    slope = jnp.where(row >= SUB, slopes_ref[PAIR * p + 1], slopes_ref[PAIR * p])
    before_start = jnp.where(col < SUB, jnp.where(first, NEG_BIG, 0.0), 0.0)
    for n, d in enumerate(DILATIONS):
        bias = jnp.where(in_window, -slope * (dist * d).astype(F32), NEG_BIG)
        bias_ref[2 * n] = bias
        bias_ref[2 * n + 1] = bias + before_start

    Q, KP, KC, VP, VC = range(5)
    token_order = (q_ref, kp_ref, kc_ref, vp_ref, vc_ref)
    quarter = SUPER // 4
    for a, src in enumerate(token_order):
        for c in range(4):
            by4_ref[a, c * quarter:(c + 1) * quarter, :] = src[0, pl.ds(c, quarter, stride=4), :]

    def rows(a, start, d):
        if d == 1:
            return token_order[a][0, pl.ds(start, SUB), :]
        first_row = (start % 4) * quarter + start // 4
        if d == 4:
            return by4_ref[a, pl.ds(first_row, SUB), :]
        return by4_ref[a, pl.ds(first_row, SUB, stride=d // 4), :]

    def stage(n, start, at_edge):
        d = DILATIONS[n]
        if at_edge:
            k_lo, v_lo = rows(KP, start + SUPER - SUB * d, d), rows(VP, start + SUPER - SUB * d, d)
        else:
            k_lo, v_lo = rows(KC, start - SUB * d, d), rows(VC, start - SUB * d, d)
        kk = jnp.concatenate([k_lo, rows(KC, start, d)], axis=0).astype(BF16)
        vv = jnp.concatenate([v_lo, rows(VC, start, d)], axis=0).astype(BF16)
        return _stack_heads(rows(Q, start, d).astype(BF16)), kk, vv

    def softmax_pv(n, at_edge, qk, vv):
        sc = qk + bias_ref[2 * n + (1 if at_edge else 0)]
        m = jnp.max(sc, axis=-1, keepdims=True)
        e = jnp.exp(sc - m)
        l = jnp.sum(e, axis=-1, keepdims=True)
        o = _dot(e.astype(BF16), vv) / l
        lane = lax.broadcasted_iota(jnp.int32, (SUB, LANES), 1)
        return jnp.where(lane < HEAD_DIM, o[:SUB], o[SUB:]), _unstack_heads(m + jnp.log(l))

    def scatter(n, start, o, lse):
        o_s, l_s = (o4_ref, l4_ref) if n == 1 else (o16_ref, l16_ref)
        o_s[pl.ds(start, SUB, stride=DILATIONS[n]), :] = o
        l_s[pl.ds(start, SUB, stride=DILATIONS[n]), :] = lse

    def merge(n, start, o1, l1):
        span = pl.ds(start, SUB)
        l4, l16 = l4_ref[span, :], l16_ref[span, :]
        top = jnp.maximum(jnp.maximum(l1, l4), l16)
        w1, w4, w16 = jnp.exp(l1 - top), jnp.exp(l4 - top), jnp.exp(l16 - top)
        o = (w1 * o1 + w4 * o4_ref[span, :] + w16 * o16_ref[span, :]) / (w1 + w4 + w16)
        lane = lax.broadcasted_iota(jnp.int32, (SUB, LANES), 1)
        sq = o * o
        ms = [jnp.mean(jnp.where(keep, sq, 0.0), axis=-1, keepdims=True) * PAIR
              for keep in (lane < HEAD_DIM, lane >= HEAD_DIM)]
        scale = jnp.where(lane < HEAD_DIM, lax.rsqrt(ms[0] + RMS_EPS), lax.rsqrt(ms[1] + RMS_EPS))
        o_ref[0, span, :] = (o * scale * g_ref[...]).astype(o_ref.dtype)

    work = [(2, r, True, scatter) for r in range(DILATIONS[2])]
    work += [(1, j * SUB * DILATIONS[1] + r, j == 0, scatter)
             for j in range(SUPER // (SUB * DILATIONS[1])) for r in range(DILATIONS[1])]
    work += [(0, j * SUB, j == 0, merge) for j in range(SUPER // SUB)]

    def front(item):
        q2, kk, vv = stage(*item[:3])
        return _dot_nt(q2, kk), vv

    fronts = [front(item) for item in work[:GROUP]]
    for t, (n, start, edge, sink) in enumerate(work):
        if t + GROUP < len(work):
            fronts.append(front(work[t + GROUP]))
        qk, vv = fronts[t]
        sink(n, start, *softmax_pv(n, edge, qk, vv))


def _dil_attention(qkv, slopes, gain):
    b, s, _ = qkv.shape
    n_pairs = gain.shape[1] // LANES
    cur = lambda c: (lambda bi, p, i, sl: (bi, i, c * n_pairs + p))
    prev = lambda c: (lambda bi, p, i, sl: (bi, jnp.maximum(i - 1, 0), c * n_pairs + p))
    blk = (1, SUPER, LANES)
    work = pltpu.VMEM((SUPER, LANES), F32)
    return pl.pallas_call(
        _dil_kernel,
        grid_spec=pltpu.PrefetchScalarGridSpec(
            num_scalar_prefetch=1,
            grid=(b, n_pairs, s // SUPER),
            in_specs=[pl.BlockSpec(blk, cur(0)),
                      pl.BlockSpec(blk, prev(1)), pl.BlockSpec(blk, cur(1)),
                      pl.BlockSpec(blk, prev(2)), pl.BlockSpec(blk, cur(2)),
                      pl.BlockSpec((1, LANES), lambda bi, p, i, sl: (0, p))],
            out_specs=pl.BlockSpec(blk, cur(0)),
            scratch_shapes=[pltpu.VMEM((2 * len(DILATIONS), 2 * SUB, 2 * SUB), F32),
                            pltpu.VMEM((5, SUPER, LANES), F32), work, work, work, work]),
        out_shape=jax.ShapeDtypeStruct((b, s, n_pairs * LANES), BF16),
        compiler_params=_params(("parallel", "parallel", "arbitrary")),
        name="dil_attn",
    )(slopes, qkv, qkv, qkv, qkv, qkv, gain)


def _post_kernel(x_ref, ma_ref, mb_ref, wo_ref, g1_ref, b1_ref, wg_ref, wu_ref, wd_ref, g2_ref,
                 b2_ref, o_ref, act_ref, *, alpha, chunk, parts):
    rows = x_ref.shape[0] // parts
    spans = [slice(r * rows, (r + 1) * rows) for r in range(parts)]
    mixes = [_dot(jnp.concatenate([ma_ref[sp, :], mb_ref[sp, :]], axis=1), wo_ref[...])
             for sp in spans]
    hs = [_layer_norm(alpha * x_ref[sp, :] + mix, g1_ref[...], b1_ref[...])
          for sp, mix in zip(spans, mixes)]
    for r, (sp, h) in enumerate(zip(spans, hs)):
        hb = h.astype(BF16)
        for c in range(act_ref.shape[2] // chunk):
            cols = slice(c * chunk, (c + 1) * chunk)
            gate = _dot(hb, wg_ref[:, cols])
            up = _dot(hb, wu_ref[:, cols])
            act_ref[r, :, cols] = (gate / (1.0 + jnp.exp(-gate)) * up).astype(BF16)
        ffn = _dot(act_ref[r], wd_ref[...])
        o_ref[sp, :] = _layer_norm(alpha * h + ffn, g2_ref[...], b2_ref[...])


def _post(x2d, ma, mb, w_out, ln1_g, ln1_b, w_gate, w_up, w_down, ln2_g, ln2_b, alpha, tm, chunk,
          parts):
    n, d = x2d.shape
    d_ff = w_gate.shape[1]
    row = lambda w: pl.BlockSpec((tm, w), lambda i: (i, 0))
    const = lambda a: pl.BlockSpec(a.shape, lambda i: (0, 0), pipeline_mode=pl.Buffered(1))
    consts = (w_out, ln1_g, ln1_b, w_gate, w_up, w_down, ln2_g, ln2_b)
    return pl.pallas_call(
        functools.partial(_post_kernel, alpha=alpha, chunk=chunk, parts=parts),
        grid=(n // tm,),
        in_specs=[row(d), row(ma.shape[1]), row(mb.shape[1])] + [const(a) for a in consts],
        out_specs=row(d),
        out_shape=jax.ShapeDtypeStruct((n, d), F32),
        scratch_shapes=[pltpu.VMEM((parts, tm // parts, d_ff), BF16)],
        compiler_params=_params(("parallel",), POST_VMEM_LIMIT),
        name="post",
    )(x2d, ma, mb, *consts)


def kernel(x, w_in, g_sb, g_dil, w_out, ln1_g, ln1_b, w_gate, w_up, w_down, ln2_g, ln2_b):
    b, s, d = x.shape
    depth = w_in.shape[0]
    sb_width = g_sb.shape[1]
    dil_width = g_dil.shape[1]
    n_dil_heads = dil_width // HEAD_DIM
    alpha = (2.0 * depth) ** 0.25
    slopes = jnp.exp2(-8.0 * (jnp.arange(n_dil_heads, dtype=F32) + 1.0) / n_dil_heads)
    assert s % SUPER == 0

    h = x.reshape(b * s, d)
    for l in range(depth):
        w = w_in[l].astype(BF16)
        w_vt = w[:, 2 * sb_width:3 * sb_width].T
        qk_a, vt_a, qkv_b = _in_proj(h, w, w_vt, wa=2 * sb_width, tm=ROW_TILE)
        mixed_a = _sb_attention(qk_a.reshape(b, s, -1), vt_a, g_sb[l][None, :], tq=SB_BLOCK,
                                width=SB_WIDTH, blocks=SB_STEP_BLOCKS)
        mixed_b = _dil_attention(qkv_b.reshape(b, s, -1), slopes, g_dil[l][None, :])
        h = _post(h, mixed_a.reshape(b * s, sb_width), mixed_b.reshape(b * s, dil_width),
                  w_out[l].astype(BF16), ln1_g[l][None, :], ln1_b[l][None, :],
                  w_gate[l].astype(BF16), w_up[l].astype(BF16), w_down[l].astype(BF16),
                  ln2_g[l][None, :], ln2_b[l][None, :], alpha, tm=ROW_TILE, chunk=FFN_CHUNK,
                  parts=POST_PARTS)
    return h.reshape(b, s, d)
```

```python
import functools

import jax
import jax.numpy as jnp
from jax import lax
from jax.experimental import pallas as pl
from jax.experimental.pallas import tpu as pltpu

HEAD_DIM = 64
LANES = 128
PAIR = LANES // HEAD_DIM
DILATIONS = (1, 4, 16)
WINDOW_KEYS = 128
SUB = WINDOW_KEYS
SUPER = SUB * max(DILATIONS)
GROUP = 2
ROW_TILE = 1024
POST_PARTS = 8
FFN_CHUNK = 256
SB_BLOCK = 256
SB_WIDTH = 4 * LANES
SB_STEP_BLOCKS = 4
LN_EPS = 1e-5
RMS_EPS = 1e-6
NEG_BIG = -1e30
F32_EXP_UNDERFLOW = -104.0
LOG2_E = 1.4426950408889634
VMEM_LIMIT = 48 * 1024 * 1024
POST_VMEM_LIMIT = 58 * 1024 * 1024

F32 = jnp.float32
BF16 = jnp.bfloat16


def _dot(a, b):
    return jnp.dot(a, b, preferred_element_type=F32)


def _dot_nt(a, b):
    return lax.dot_general(a, b, (((1,), (1,)), ((), ())), preferred_element_type=F32)


def _layer_norm(y, g, b):
    mu = jnp.mean(y, axis=-1, keepdims=True)
    d = y - mu
    var = jnp.mean(d * d, axis=-1, keepdims=True)
    return d * lax.rsqrt(var + LN_EPS) * g + b


def _stack_heads(q):
    lane = lax.broadcasted_iota(jnp.int32, q.shape, 1)
    q = q * jnp.asarray(HEAD_DIM ** -0.5, BF16)
    zero = jnp.zeros_like(q)
    return jnp.concatenate([jnp.where(lane < HEAD_DIM, q, zero),
                            jnp.where(lane < HEAD_DIM, zero, q)], axis=0)


def _unstack_heads(x):
    rows = x.shape[0] // PAIR
    lane = lax.broadcasted_iota(jnp.int32, (rows, LANES), 1)
    return jnp.where(lane < HEAD_DIM, jnp.broadcast_to(x[:rows], (rows, LANES)),
                     jnp.broadcast_to(x[rows:], (rows, LANES)))


def _exp(sign, x):
    return jnp.exp2(x * (sign * LOG2_E))


def _params(sem, vmem_limit=VMEM_LIMIT):
    return pltpu.CompilerParams(dimension_semantics=sem, vmem_limit_bytes=vmem_limit)


def _in_proj_kernel(x_ref, wa_ref, wvt_ref, wb_ref, a_ref, vt_ref, b_ref):
    xb = x_ref[...].astype(BF16)
    a_ref[...] = _dot(xb, wa_ref[...]).astype(a_ref.dtype)
    vt_ref[...] = _dot_nt(wvt_ref[...], xb).astype(vt_ref.dtype)
    b_ref[...] = _dot(xb, wb_ref[...])


def _in_proj(x2d, w, wvt, wa, tm):
    n, d = x2d.shape
    wv = wvt.shape[0]
    wb = w.shape[1] - wa - wv
    assert (wa + wv) % wb == 0
    return pl.pallas_call(
        _in_proj_kernel,
        grid=(n // tm,),
        in_specs=[pl.BlockSpec((tm, d), lambda i: (i, 0)),
                  pl.BlockSpec((d, wa), lambda i: (0, 0)),
                  pl.BlockSpec(wvt.shape, lambda i: (0, 0)),
                  pl.BlockSpec((d, wb), lambda i: (0, (wa + wv) // wb))],
        out_specs=[pl.BlockSpec((tm, wa), lambda i: (i, 0)),
                   pl.BlockSpec((wv, tm), lambda i: (0, i)),
                   pl.BlockSpec((tm, wb), lambda i: (i, 0))],
        out_shape=[jax.ShapeDtypeStruct((n, wa), BF16), jax.ShapeDtypeStruct((wv, n), BF16),
                   jax.ShapeDtypeStruct((n, wb), F32)],
        compiler_params=_params(("parallel",)),
        name="in_proj",
    )(x2d, w, wvt, w)


def _sb_kernel(q_ref, k_ref, vt_ref, ntri_ref, g_ref, o_ref, acc_ref, carry_ref, *, tq):
    pairs = range(q_ref.shape[2] // LANES)
    cols = [slice(t * LANES, (t + 1) * LANES) for t in pairs]
    blocks = q_ref.shape[1] // tq
    for qb in range(blocks):
        _sb_block(pl.program_id(2) * blocks + qb, slice(qb * tq, (qb + 1) * tq), pairs, cols,
                  q_ref, k_ref, vt_ref, ntri_ref, g_ref, o_ref, acc_ref, carry_ref, tq)


def _sb_block(i, span, pairs, cols, q_ref, k_ref, vt_ref, ntri_ref, g_ref, o_ref, acc_ref,
              carry_ref, tq):
    q2 = [_stack_heads(q_ref[0, span, cols[t]]) for t in pairs]

    acc_ref[...] = jnp.zeros_like(acc_ref)
    carry_ref[...] = jnp.zeros_like(carry_ref)

    def tile(kb, diagonal):
        start = pl.multiple_of(kb * tq, tq)
        if diagonal:
            key = lax.broadcasted_iota(jnp.int32, (tq, 2 * tq), 0)
            col = lax.broadcasted_iota(jnp.int32, (tq, 2 * tq), 1)
            keep = key < jnp.where(col >= tq, col - tq, col)
        k_blks = [k_ref[0, pl.ds(start, tq), cols[t]] for t in pairs]
        vt_blks = [vt_ref[cols[t], pl.ds(start, tq)] for t in pairs]
        ntri = ntri_ref[...]
        acc_old = [acc_ref[t] for t in pairs]
        carry_old = [carry_ref[t] for t in pairs]
        zs = [_dot_nt(k_blks[t], q2[t]) for t in pairs]
        log_betas, afters, carry_new = [], [], []
        for t in pairs:
            z = zs[t]
            stay = jnp.maximum(z, 0.0) + jnp.log(1.0 + _exp(-1.0, jnp.abs(z)))
            log_betas.append(z - stay)
            if diagonal:
                stay = jnp.where(keep, stay, 0.0)
            hi = stay.astype(BF16)
            lo = (stay - hi.astype(F32)).astype(BF16)
            afters.append(_dot(ntri, jnp.concatenate([hi, lo], axis=0)))
            carry_new.append(carry_old[t] - jnp.sum(stay, axis=0, keepdims=True))
        for t in pairs:
            w = _exp(1.0, log_betas[t] + afters[t] + carry_old[t])
            if diagonal:
                w = jnp.where(keep, w, 0.0)
            acc_ref[t] = acc_old[t] + _dot(vt_blks[t], w.astype(BF16))
            carry_ref[t] = carry_new[t]

    tile(i, True)

    def more(state):
        kb, top = state
        return jnp.logical_and(kb >= 0, top > F32_EXP_UNDERFLOW)

    def body(state):
        kb, _ = state
        tile(kb, False)
        return kb - 1, jnp.max(carry_ref[...])

    lax.while_loop(more, body, (i - 1, jnp.zeros((), F32)))

    feature = lax.broadcasted_iota(jnp.int32, (LANES, tq), 0)
    for t in pairs:
        acc = acc_ref[t]
        heads = (acc[:HEAD_DIM, :tq], acc[HEAD_DIM:, tq:])
        scale = [lax.rsqrt(jnp.mean(h * h, axis=0, keepdims=True) + RMS_EPS) for h in heads]
        o = jnp.where(feature < HEAD_DIM, acc[:, :tq] * scale[0], acc[:, tq:] * scale[1])
        o_ref[0, span, cols[t]] = (o.T * g_ref[:, cols[t]]).astype(o_ref.dtype)


def _sb_attention(qk, vt, gain, tq, width, blocks):
    b, s, _ = qk.shape
    groups = gain.shape[1] // width
    ntri = -(jnp.arange(tq)[None, :] > jnp.arange(tq)[:, None]).astype(BF16)
    ntri = jnp.concatenate([ntri, ntri], axis=1)
    return pl.pallas_call(
        functools.partial(_sb_kernel, tq=tq),
        grid=(b, groups, s // (blocks * tq)),
        in_specs=[pl.BlockSpec((1, blocks * tq, width), lambda bi, p, i: (bi, i, p)),
                  pl.BlockSpec((1, s, width), lambda bi, p, i: (bi, 0, groups + p)),
                  pl.BlockSpec((width, s), lambda bi, p, i: (p, bi)),
                  pl.BlockSpec((tq, 2 * tq), lambda bi, p, i: (0, 0)),
                  pl.BlockSpec((1, width), lambda bi, p, i: (0, p))],
        out_specs=pl.BlockSpec((1, blocks * tq, width), lambda bi, p, i: (bi, i, p)),
        out_shape=jax.ShapeDtypeStruct((b, s, groups * width), BF16),
        scratch_shapes=[pltpu.VMEM((width // LANES, LANES, 2 * tq), F32),
                        pltpu.VMEM((width // LANES, 1, 2 * tq), F32)],
        compiler_params=_params(("parallel", "parallel", "arbitrary")),
        name="sb_attn",
    )(qk, qk, vt, ntri, gain)


def _dil_kernel(slopes_ref, q_ref, kp_ref, kc_ref, vp_ref, vc_ref, g_ref, o_ref,
                bias_ref, by4_ref, o4_ref, l4_ref, o16_ref, l16_ref):
    p = pl.program_id(1)
    first = pl.program_id(2) == 0

    row = lax.broadcasted_iota(jnp.int32, (2 * SUB, 2 * SUB), 0)
    col = lax.broadcasted_iota(jnp.int32, (2 * SUB, 2 * SUB), 1)
    dist = jnp.where(row >= SUB, row - SUB, row) + SUB - col
    in_window = (dist >= 0) & (dist <= WINDOW_KEYS)
    slope = jnp.where(row >= SUB, slopes_ref[PAIR * p + 1], slopes_ref[PAIR * p])
    before_start = jnp.where(col < SUB, jnp.where(first, NEG_BIG, 0.0), 0.0)
    for n, d in enumerate(DILATIONS):
        bias = jnp.where(in_window, -slope * (dist * d).astype(F32), NEG_BIG)
        bias_ref[2 * n] = bias
        bias_ref[2 * n + 1] = bias + before_start

    Q, KP, KC, VP, VC = range(5)
    token_order = (q_ref, kp_ref, kc_ref, vp_ref, vc_ref)
    quarter = SUPER // 4
    for a, src in enumerate(token_order):
        for c in range(4):
            by4_ref[a, c * quarter:(c + 1) * quarter, :] = src[0, pl.ds(c, quarter, stride=4), :]

    def rows(a, start, d):
        if d == 1:
            return token_order[a][0, pl.ds(start, SUB), :]
        first_row = (start % 4) * quarter + start // 4
        if d == 4:
            return by4_ref[a, pl.ds(first_row, SUB), :]
        return by4_ref[a, pl.ds(first_row, SUB, stride=d // 4), :]

    def stage(n, start, at_edge):
        d = DILATIONS[n]
        if at_edge:
            k_lo, v_lo = rows(KP, start + SUPER - SUB * d, d), rows(VP, start + SUPER - SUB * d, d)
        else:
            k_lo, v_lo = rows(KC, start - SUB * d, d), rows(VC, start - SUB * d, d)
        kk = jnp.concatenate([k_lo, rows(KC, start, d)], axis=0).astype(BF16)
        vv = jnp.concatenate([v_lo, rows(VC, start, d)], axis=0).astype(BF16)
        return _stack_heads(rows(Q, start, d).astype(BF16)), kk, vv

    def softmax_pv(n, at_edge, qk, vv):
        sc = qk + bias_ref[2 * n + (1 if at_edge else 0)]
        m = jnp.max(sc, axis=-1, keepdims=True)
        e = jnp.exp(sc - m)
        l = jnp.sum(e, axis=-1, keepdims=True)
        o = _dot(e.astype(BF16), vv) / l
        lane = lax.broadcasted_iota(jnp.int32, (SUB, LANES), 1)
        return jnp.where(lane < HEAD_DIM, o[:SUB], o[SUB:]), _unstack_heads(m + jnp.log(l))

    def scatter(n, start, o, lse):
        o_s, l_s = (o4_ref, l4_ref) if n == 1 else (o16_ref, l16_ref)
        o_s[pl.ds(start, SUB, stride=DILATIONS[n]), :] = o
        l_s[pl.ds(start, SUB, stride=DILATIONS[n]), :] = lse

    def merge(n, start, o1, l1):
        span = pl.ds(start, SUB)
        l4, l16 = l4_ref[span, :], l16_ref[span, :]
        top = jnp.maximum(jnp.maximum(l1, l4), l16)
        w1, w4, w16 = jnp.exp(l1 - top), jnp.exp(l4 - top), jnp.exp(l16 - top)
        o = (w1 * o1 + w4 * o4_ref[span, :] + w16 * o16_ref[span, :]) / (w1 + w4 + w16)
        lane = lax.broadcasted_iota(jnp.int32, (SUB, LANES), 1)
        sq = o * o
        ms = [jnp.mean(jnp.where(keep, sq, 0.0), axis=-1, keepdims=True) * PAIR
              for keep in (lane < HEAD_DIM, lane >= HEAD_DIM)]
        scale = jnp.where(lane < HEAD_DIM, lax.rsqrt(ms[0] + RMS_EPS), lax.rsqrt(ms[1] + RMS_EPS))
        o_ref[0, span, :] = (o * scale * g_ref[...]).astype(o_ref.dtype)

    work = [(2, r, True, scatter) for r in range(DILATIONS[2])]
    work += [(1, j * SUB * DILATIONS[1] + r, j == 0, scatter)
             for j in range(SUPER // (SUB * DILATIONS[1])) for r in range(DILATIONS[1])]
    work += [(0, j * SUB, j == 0, merge) for j in range(SUPER // SUB)]

    def front(item):
        q2, kk, vv = stage(*item[:3])
        return _dot_nt(q2, kk), vv

    fronts = [front(item) for item in work[:GROUP]]
    for t, (n, start, edge, sink) in enumerate(work):
        if t + GROUP < len(work):
            fronts.append(front(work[t + GROUP]))
        qk, vv = fronts[t]
        sink(n, start, *softmax_pv(n, edge, qk, vv))


def _dil_attention(qkv, slopes, gain):
    b, s, _ = qkv.shape
    n_pairs = gain.shape[1] // LANES
    cur = lambda c: (lambda bi, p, i, sl: (bi, i, c * n_pairs + p))
    prev = lambda c: (lambda bi, p, i, sl: (bi, jnp.maximum(i - 1, 0), c * n_pairs + p))
    blk = (1, SUPER, LANES)
    work = pltpu.VMEM((SUPER, LANES), F32)
    return pl.pallas_call(
        _dil_kernel,
        grid_spec=pltpu.PrefetchScalarGridSpec(
            num_scalar_prefetch=1,
            grid=(b, n_pairs, s // SUPER),
            in_specs=[pl.BlockSpec(blk, cur(0)),
                      pl.BlockSpec(blk, prev(1)), pl.BlockSpec(blk, cur(1)),
                      pl.BlockSpec(blk, prev(2)), pl.BlockSpec(blk, cur(2)),
                      pl.BlockSpec((1, LANES), lambda bi, p, i, sl: (0, p))],
            out_specs=pl.BlockSpec(blk, cur(0)),
            scratch_shapes=[pltpu.VMEM((2 * len(DILATIONS), 2 * SUB, 2 * SUB), F32),
                            pltpu.VMEM((5, SUPER, LANES), F32), work, work, work, work]),
        out_shape=jax.ShapeDtypeStruct((b, s, n_pairs * LANES), BF16),
        compiler_params=_params(("parallel", "parallel", "arbitrary")),
        name="dil_attn",
    )(slopes, qkv, qkv, qkv, qkv, qkv, gain)


def _post_kernel(x_ref, ma_ref, mb_ref, wo_ref, g1_ref, b1_ref, wg_ref, wu_ref, wd_ref, g2_ref,
                 b2_ref, o_ref, act_ref, *, alpha, chunk, parts):
    rows = x_ref.shape[0] // parts
    spans = [slice(r * rows, (r + 1) * rows) for r in range(parts)]
    mixes = [_dot(jnp.concatenate([ma_ref[sp, :], mb_ref[sp, :]], axis=1), wo_ref[...])
             for sp in spans]
    hs = [_layer_norm(alpha * x_ref[sp, :] + mix, g1_ref[...], b1_ref[...])
          for sp, mix in zip(spans, mixes)]
    for r, (sp, h) in enumerate(zip(spans, hs)):
        hb = h.astype(BF16)
        for c in range(act_ref.shape[2] // chunk):
            cols = slice(c * chunk, (c + 1) * chunk)
            gate = _dot(hb, wg_ref[:, cols])
            up = _dot(hb, wu_ref[:, cols])
            act_ref[r, :, cols] = (gate / (1.0 + jnp.exp(-gate)) * up).astype(BF16)
        ffn = _dot(act_ref[r], wd_ref[...])
        o_ref[sp, :] = _layer_norm(alpha * h + ffn, g2_ref[...], b2_ref[...])


def _post(x2d, ma, mb, w_out, ln1_g, ln1_b, w_gate, w_up, w_down, ln2_g, ln2_b, alpha, tm, chunk,
          parts):
    n, d = x2d.shape
    d_ff = w_gate.shape[1]
    row = lambda w: pl.BlockSpec((tm, w), lambda i: (i, 0))
    const = lambda a: pl.BlockSpec(a.shape, lambda i: (0, 0), pipeline_mode=pl.Buffered(1))
    consts = (w_out, ln1_g, ln1_b, w_gate, w_up, w_down, ln2_g, ln2_b)
    return pl.pallas_call(
        functools.partial(_post_kernel, alpha=alpha, chunk=chunk, parts=parts),
        grid=(n // tm,),
        in_specs=[row(d), row(ma.shape[1]), row(mb.shape[1])] + [const(a) for a in consts],
        out_specs=row(d),
        out_shape=jax.ShapeDtypeStruct((n, d), F32),
        scratch_shapes=[pltpu.VMEM((parts, tm // parts, d_ff), BF16)],
        compiler_params=_params(("parallel",), POST_VMEM_LIMIT),
        name="post",
    )(x2d, ma, mb, *consts)


def kernel(x, w_in, g_sb, g_dil, w_out, ln1_g, ln1_b, w_gate, w_up, w_down, ln2_g, ln2_b):
    b, s, d = x.shape
    depth = w_in.shape[0]
    sb_width = g_sb.shape[1]
    dil_width = g_dil.shape[1]
    n_dil_heads = dil_width // HEAD_DIM
    alpha = (2.0 * depth) ** 0.25
    slopes = jnp.exp2(-8.0 * (jnp.arange(n_dil_heads, dtype=F32) + 1.0) / n_dil_heads)
    assert s % SUPER == 0

    h = x.reshape(b * s, d)
    for l in range(depth):
        w = w_in[l].astype(BF16)
        w_vt = w[:, 2 * sb_width:3 * sb_width].T
        qk_a, vt_a, qkv_b = _in_proj(h, w, w_vt, wa=2 * sb_width, tm=ROW_TILE)
        mixed_a = _sb_attention(qk_a.reshape(b, s, -1), vt_a, g_sb[l][None, :], tq=SB_BLOCK,
                                width=SB_WIDTH, blocks=SB_STEP_BLOCKS)
        mixed_b = _dil_attention(qkv_b.reshape(b, s, -1), slopes, g_dil[l][None, :])
        h = _post(h, mixed_a.reshape(b * s, sb_width), mixed_b.reshape(b * s, dil_width),
                  w_out[l].astype(BF16), ln1_g[l][None, :], ln1_b[l][None, :],
                  w_gate[l].astype(BF16), w_up[l].astype(BF16), w_down[l].astype(BF16),
                  ln2_g[l][None, :], ln2_b[l][None, :], alpha, tm=ROW_TILE, chunk=FFN_CHUNK,
                  parts=POST_PARTS)
    return h.reshape(b, s, d)
```
